```python
import math
import jax, jax.numpy as jnp
from jax import lax
import numpy as np

D_MODEL = 1024
BATCH = 16
SEQ = 2048
DEPTH = 1

GRID_W = 64
CTX_LEN = 256
N_HEADS = 16
N_KV_HEADS = 4
HEAD_DIM = 64
KV_REP = N_HEADS // N_KV_HEADS
ATTN_W = N_HEADS * HEAD_DIM
KV_W = N_KV_HEADS * HEAD_DIM
Q_BLOCK = 128
ROPE_THETA = 10000.0
ROPE_AXIS_DIM = HEAD_DIM // 2
D_SSM = D_MODEL // 2
SSM_GROUP = 16
N_SSM_GROUPS = D_SSM // SSM_GROUP
SSM_STATE = 64
DT_MIN = 1e-3
DT_MAX = 1e-1
D_FF = 4 * D_MODEL
NORM_EPS = 1e-6
OFF_Q = 0
OFF_K = OFF_Q + ATTN_W
OFF_V = OFF_K + KV_W
OFF_U = OFF_V + KV_W
OFF_GA = OFF_U + D_SSM
OFF_GS = OFF_GA + D_MODEL
W_IN_COLS = OFF_GS + D_MODEL

kernel_name = 'hybrid_s5_gqa_gated_dit_block'


def rms_norm(x, g):
    xf = x.astype(jnp.float32)
    xf = xf * lax.rsqrt(jnp.mean(xf * xf, axis=-1, keepdims=True) + NORM_EPS)
    return xf.astype(x.dtype) * g


def modulate(h, shift, scale):
    return h * (1 + scale) + shift


def axial_rope_tables(n_tok):
    rows = n_tok // GRID_W
    row = jnp.repeat(jnp.arange(rows, dtype=jnp.float32), GRID_W)
    col = jnp.tile(jnp.arange(GRID_W, dtype=jnp.float32), rows)
    half = ROPE_AXIS_DIM // 2
    inv_freq = ROPE_THETA ** (-jnp.arange(half, dtype=jnp.float32) / half)
    ang = jnp.concatenate([row[:, None] * inv_freq, col[:, None] * inv_freq], axis=-1)
    return jnp.cos(ang), jnp.sin(ang)


def apply_axial_rope(x, cos, sin):
    xf = x.astype(jnp.float32)
    half = ROPE_AXIS_DIM // 2
    parts = []
    for axis_idx in range(2):
        xa = xf[..., axis_idx * ROPE_AXIS_DIM:(axis_idx + 1) * ROPE_AXIS_DIM]
        cs = cos[None, :, None, axis_idx * half:(axis_idx + 1) * half]
        sn = sin[None, :, None, axis_idx * half:(axis_idx + 1) * half]
        x1, x2 = xa[..., :half], xa[..., half:]
        parts += [x1 * cs - x2 * sn, x2 * cs + x1 * sn]
    return jnp.concatenate(parts, axis=-1).astype(x.dtype)


def attend(q, k, v):
    bsz, lq = q.shape[:2]
    nblk = lq // Q_BLOCK
    qb = q.reshape(bsz, nblk, Q_BLOCK, N_KV_HEADS, KV_REP, HEAD_DIM).transpose(1, 0, 2, 3, 4, 5)
    scale = HEAD_DIM ** -0.5

    def one_block(qblk):
        s = jnp.einsum('bqgrd,bkgd->bgrqk', qblk, k).astype(jnp.float32) * scale
        p = jax.nn.softmax(s, axis=-1).astype(v.dtype)
        return jnp.einsum('bgrqk,bkgd->bqgrd', p, v)

    o = lax.map(one_block, qb)
    return o.transpose(1, 0, 2, 3, 4, 5).reshape(bsz, lq, ATTN_W)


def ssm_discretize(lam_re, lam_im, log_dt, b_re, b_im):
    lam = lax.complex(lam_re.astype(jnp.float32), lam_im.astype(jnp.float32))
    dt = jnp.exp(log_dt.astype(jnp.float32))[:, None]
    a_bar = jnp.exp(lam * dt)
    b = lax.complex(b_re.astype(jnp.float32), b_im.astype(jnp.float32))
    b_bar = ((a_bar - 1) / lam)[..., None] * b
    return a_bar, b_bar


def _linear_recurrence_op(e1, e2):
    a1, b1 = e1
    a2, b2 = e2
    return a1 * a2, a2 * b1 + b2


def ssm_scan(u, a_bar, b_bar, h0, reverse):
    bsz, n = u.shape[:2]
    ug = u.astype(jnp.float32).reshape(bsz, n, N_SSM_GROUPS, SSM_GROUP).astype(jnp.complex64)
    bu = jnp.einsum('blgh,gph->blgp', ug, b_bar)
    edge = n - 1 if reverse else 0
    bu = bu.at[:, edge].add(a_bar * h0)
    a = jnp.broadcast_to(a_bar, bu.shape)
    _, states = lax.associative_scan(_linear_recurrence_op, (a, bu), reverse=reverse, axis=1)
    return states


def ssm_readout(states, c_re, c_im):
    bsz, n = states.shape[:2]
    cmat = lax.complex(c_re.astype(jnp.float32), c_im.astype(jnp.float32))
    y = jnp.einsum('blgp,ghp->blgh', states, cmat).real
    return y.reshape(bsz, n, D_SSM)


def merge_branches(attn_o, y_ssm, gate_a, gate_s, w_glu, b_glu, w_br_attn, w_br_ssm, w_out):
    y = jax.nn.gelu(y_ssm)
    y = y * jax.nn.sigmoid(y @ w_glu + b_glu)
    merged = jax.nn.sigmoid(gate_a) * (attn_o @ w_br_attn) + jax.nn.sigmoid(gate_s) * (y @ w_br_ssm)
    return merged @ w_out


def sq_relu_mlp(h, w1, b1, w2, b2):
    return jnp.square(jax.nn.relu(h @ w1 + b1)) @ w2 + b2


def _cols(p, base, start, stop):
    return p[..., start - base:stop - base]


def setup_inputs(seed: int = 0) -> dict:
    key = jax.random.key(seed)
    ks = jax.random.split(key, 32)
    nrm = lambda k, shape, s: jax.random.normal(k, shape, jnp.float32) * s
    G, P, H = N_SSM_GROUPS, SSM_STATE, SSM_GROUP
    lam_im_base = jnp.pi * jnp.arange(P, dtype=jnp.float32)
    return {
        'x': nrm(ks[0], (BATCH, SEQ, D_MODEL), 1.0),
        'c': nrm(ks[1], (BATCH, D_MODEL), 1.0),
        'ctx': nrm(ks[2], (BATCH, CTX_LEN, D_MODEL), 1.0),
        'c_ctx': nrm(ks[3], (D_MODEL,), 1.0),
        'w_mod': nrm(ks[4], (DEPTH, D_MODEL, 6 * D_MODEL), 0.5 * D_MODEL ** -0.5),
        'b_mod': nrm(ks[5], (DEPTH, 6 * D_MODEL), 0.01),
        'norm1_g': 1.0 + nrm(ks[6], (DEPTH, D_MODEL), 0.02),
        'norm2_g': 1.0 + nrm(ks[7], (DEPTH, D_MODEL), 0.02),
        'w_in': nrm(ks[8], (DEPTH, D_MODEL, W_IN_COLS), D_MODEL ** -0.5),
        'q_norm_g': 1.0 + nrm(ks[9], (DEPTH, HEAD_DIM), 0.02),
        'k_norm_g': 1.0 + nrm(ks[10], (DEPTH, HEAD_DIM), 0.02),
        'ssm_lambda_re': -0.5 + nrm(ks[11], (DEPTH, 2, G, P), 0.01),
        'ssm_lambda_im': lam_im_base + nrm(ks[12], (DEPTH, 2, G, P), 0.01),
        'ssm_log_dt': jax.random.uniform(ks[13], (DEPTH, 2, G), jnp.float32, math.log(DT_MIN), math.log(DT_MAX)),
        'ssm_b_re': nrm(ks[14], (DEPTH, 2, G, P, H), (2 * H) ** -0.5),
        'ssm_b_im': nrm(ks[15], (DEPTH, 2, G, P, H), (2 * H) ** -0.5),
        'ssm_c_re': nrm(ks[16], (DEPTH, 2, G, H, P), (2 * P) ** -0.5),
        'ssm_c_im': nrm(ks[17], (DEPTH, 2, G, H, P), (2 * P) ** -0.5),
        'ssm_d': nrm(ks[18], (DEPTH, D_SSM), 1.0),
        'w_glu': nrm(ks[19], (DEPTH, D_SSM, D_SSM), D_SSM ** -0.5),
        'b_glu': nrm(ks[20], (DEPTH, D_SSM), 0.01),
        'w_br_attn': nrm(ks[21], (DEPTH, ATTN_W, D_MODEL), ATTN_W ** -0.5),
        'w_br_ssm': nrm(ks[22], (DEPTH, D_SSM, D_MODEL), D_SSM ** -0.5),
        'w_out': nrm(ks[23], (DEPTH, D_MODEL, D_MODEL), D_MODEL ** -0.5),
        'w_mlp1': nrm(ks[24], (DEPTH, D_MODEL, D_FF), D_MODEL ** -0.5),
        'b_mlp1': nrm(ks[25], (DEPTH, D_FF), 0.01),
        'w_mlp2': nrm(ks[26], (DEPTH, D_FF, D_MODEL), D_FF ** -0.5),
        'b_mlp2': nrm(ks[27], (DEPTH, D_MODEL), 0.01),
    }


def reference(x, c, ctx, c_ctx, w_mod, b_mod, norm1_g, norm2_g, w_in, q_norm_g, k_norm_g,
              ssm_lambda_re, ssm_lambda_im, ssm_log_dt, ssm_b_re, ssm_b_im, ssm_c_re, ssm_c_im,
              ssm_d, w_glu, b_glu, w_br_attn, w_br_ssm, w_out, w_mlp1, b_mlp1, w_mlp2, b_mlp2):
    bsz, n_tok, _ = x.shape
    n_ctx = ctx.shape[1]
    cos, sin = axial_rope_tables(n_tok)
    h0_zero = jnp.zeros((bsz, N_SSM_GROUPS, SSM_STATE), jnp.complex64)
    ctx_s = ctx
    for l in range(DEPTH):
        last = l == DEPTH - 1
        sh1_l, sc1_l, g1_l, sh2_l, sc2_l, g2_l = jnp.split(
            (jax.nn.silu(c) @ w_mod[l] + b_mod[l])[:, None, :], 6, axis=-1)
        sh1_c, sc1_c, g1_c, sh2_c, sc2_c, g2_c = jnp.split(
            jax.nn.silu(c_ctx) @ w_mod[l] + b_mod[l], 6, axis=-1)

        h_l = modulate(rms_norm(x, norm1_g[l]), sh1_l, sc1_l)
        h_c = modulate(rms_norm(ctx_s, norm1_g[l]), sh1_c, sc1_c)
        p_l = h_l @ w_in[l]
        c0, c1 = (OFF_K, OFF_GA) if last else (0, W_IN_COLS)
        p_c = h_c @ w_in[l][:, c0:c1]

        q_l = apply_axial_rope(rms_norm(p_l[..., OFF_Q:OFF_K].reshape(bsz, n_tok, N_HEADS, HEAD_DIM), q_norm_g[l]), cos, sin)
        k_l = apply_axial_rope(rms_norm(p_l[..., OFF_K:OFF_V].reshape(bsz, n_tok, N_KV_HEADS, HEAD_DIM), k_norm_g[l]), cos, sin)
        v_l = p_l[..., OFF_V:OFF_U].reshape(bsz, n_tok, N_KV_HEADS, HEAD_DIM)
        u_l = p_l[..., OFF_U:OFF_GA]
        k_c = rms_norm(_cols(p_c, c0, OFF_K, OFF_V).reshape(bsz, n_ctx, N_KV_HEADS, HEAD_DIM), k_norm_g[l])
        v_c = _cols(p_c, c0, OFF_V, OFF_U).reshape(bsz, n_ctx, N_KV_HEADS, HEAD_DIM)
        u_c = _cols(p_c, c0, OFF_U, OFF_GA)

        attn_l = attend(q_l, jnp.concatenate([k_l, k_c], axis=1), jnp.concatenate([v_l, v_c], axis=1))

        a_f, b_f = ssm_discretize(ssm_lambda_re[l, 0], ssm_lambda_im[l, 0], ssm_log_dt[l, 0], ssm_b_re[l, 0], ssm_b_im[l, 0])
        a_b, b_b = ssm_discretize(ssm_lambda_re[l, 1], ssm_lambda_im[l, 1], ssm_log_dt[l, 1], ssm_b_re[l, 1], ssm_b_im[l, 1])
        s_cf = ssm_scan(u_c, a_f, b_f, h0_zero, reverse=False)
        s_cb = ssm_scan(u_c, a_b, b_b, h0_zero, reverse=True)
        y_l = (ssm_readout(ssm_scan(u_l, a_f, b_f, s_cf[:, -1], reverse=False), ssm_c_re[l, 0], ssm_c_im[l, 0])
               + ssm_readout(ssm_scan(u_l, a_b, b_b, s_cb[:, 0], reverse=True), ssm_c_re[l, 1], ssm_c_im[l, 1]))
        y_l = y_l.astype(x.dtype) + ssm_d[l] * u_l
        mix_l = merge_branches(attn_l, y_l, p_l[..., OFF_GA:OFF_GS], p_l[..., OFF_GS:W_IN_COLS],
                               w_glu[l], b_glu[l], w_br_attn[l], w_br_ssm[l], w_out[l])

        if not last:
            q_c = rms_norm(p_c[..., OFF_Q:OFF_K].reshape(bsz, n_ctx, N_HEADS, HEAD_DIM), q_norm_g[l])
            attn_c = attend(q_c, k_c, v_c)
            y_c = ssm_readout(s_cf, ssm_c_re[l, 0], ssm_c_im[l, 0]) + ssm_readout(s_cb, ssm_c_re[l, 1], ssm_c_im[l, 1])
            y_c = y_c.astype(ctx_s.dtype) + ssm_d[l] * u_c
            mix_c = merge_branches(attn_c, y_c, p_c[..., OFF_GA:OFF_GS], p_c[..., OFF_GS:W_IN_COLS],
                                   w_glu[l], b_glu[l], w_br_attn[l], w_br_ssm[l], w_out[l])
            ctx_s = ctx_s + g1_c * mix_c
            h2_c = modulate(rms_norm(ctx_s, norm2_g[l]), sh2_c, sc2_c)
            ctx_s = ctx_s + g2_c * sq_relu_mlp(h2_c, w_mlp1[l], b_mlp1[l], w_mlp2[l], b_mlp2[l])

        x = x + g1_l * mix_l
        h2_l = modulate(rms_norm(x, norm2_g[l]), sh2_l, sc2_l)
        x = x + g2_l * sq_relu_mlp(h2_l, w_mlp1[l], b_mlp1[l], w_mlp2[l], b_mlp2[l])
    return x
```

```python
import functools

import jax
import jax.numpy as jnp
from jax import lax
from jax.experimental import pallas as pl
from jax.experimental.pallas import tpu as pltpu

F32 = jnp.float32
BF16 = jnp.bfloat16

D_MODEL = 1024
N_HEADS = 16
N_KV_HEADS = 4
HEAD_DIM = 64
KV_REP = N_HEADS // N_KV_HEADS
GRID_W = 64
ROPE_THETA = 10000.0
ROPE_AXIS_DIM = HEAD_DIM // 2
D_SSM = 512
SSM_GROUP = 16
N_SSM_GROUPS = 32
SSM_STATE = 64
N_STATE = N_SSM_GROUPS * SSM_STATE
D_FF = 4 * D_MODEL
NORM_EPS = 1e-6
OFF_K = 1024
OFF_V = 1280
OFF_U = 1536
OFF_GA = 2048

LANES = 128
SUBLANES = 8
VMEM_LIMIT_BYTES = 56 * 1024 * 1024

PROJ_ROWS = 256
ATTN_ROWS = 256
MERGE_ROWS = 256
SSM_STEPS = 64
SSM_BATCH = SUBLANES
SCAN_LANES = 1024
FF_CHUNK = 1024


def _const_spec(shape):
    nd = len(shape)
    return pl.BlockSpec(shape, lambda *_: (0,) * nd, pipeline_mode=pl.Buffered(1))


def _params(n_axes, vmem=VMEM_LIMIT_BYTES):
    return pltpu.CompilerParams(dimension_semantics=("arbitrary",) * n_axes, vmem_limit_bytes=vmem)


def _mod_body(c_ref, w_ref, b_ref, o_ref):
    c = c_ref[...]
    s = c * jax.nn.sigmoid(c)
    o_ref[...] = jnp.dot(s, w_ref[...], preferred_element_type=F32) + b_ref[...]


def _modulation(c_rows, w_mod, b_mod):
    rows = c_rows.shape[0]
    ncol = w_mod.shape[1]
    blk = D_MODEL
    return pl.pallas_call(
        _mod_body,
        grid=(ncol // blk,),
        in_specs=[pl.BlockSpec((rows, D_MODEL), lambda j: (0, 0)),
                  pl.BlockSpec((D_MODEL, blk), lambda j: (0, j)),
                  pl.BlockSpec((1, blk), lambda j: (0, j))],
        out_specs=pl.BlockSpec((rows, blk), lambda j: (0, j)),
        out_shape=jax.ShapeDtypeStruct((rows, ncol), F32),
        compiler_params=_params(1),
        name="mod",
    )(c_rows, w_mod, b_mod)


def _disc_body(lr_ref, li_ref, ldt_ref, br_ref, bi_ref, ar_ref, ai_ref, bbr_ref, bbi_ref):
    lr = lr_ref[...]
    li = li_ref[...]
    dt = jnp.exp(ldt_ref[...])
    mag = jnp.exp(lr * dt)
    ar = mag * jnp.cos(li * dt)
    ai = mag * jnp.sin(li * dt)
    nr = ar - 1.0
    den = lr * lr + li * li
    cr = (nr * lr + ai * li) / den
    ci = (ai * lr - nr * li) / den
    br = br_ref[...]
    bi = bi_ref[...]
    ar_ref[...] = ar
    ai_ref[...] = ai
    bbr_ref[...] = cr * br - ci * bi
    bbi_ref[...] = cr * bi + ci * br


def _discretize(lam_re, lam_im, log_dt, b_re, b_im):
    rows = 2 * N_SSM_GROUPS
    cols = SSM_STATE * SSM_GROUP
    rep = lambda a: jnp.repeat(a.reshape(rows, SSM_STATE), SSM_GROUP, axis=1)
    ldt = jnp.broadcast_to(log_dt.reshape(rows, 1), (rows, cols))
    spec = pl.BlockSpec((rows, cols), lambda: (0, 0))
    sds = jax.ShapeDtypeStruct((rows, cols), F32)
    ar, ai, bbr, bbi = pl.pallas_call(
        _disc_body,
        in_specs=[spec] * 5,
        out_specs=[spec] * 4,
        out_shape=[sds] * 4,
        name="disc",
    )(rep(lam_re), rep(lam_im), ldt, b_re.reshape(rows, cols), b_im.reshape(rows, cols))
    a_re = ar[:, ::SSM_GROUP].reshape(2, N_STATE)
    a_im = ai[:, ::SSM_GROUP].reshape(2, N_STATE)
    shape = (2, N_SSM_GROUPS, SSM_STATE, SSM_GROUP)
    return a_re, a_im, bbr.reshape(shape), bbi.reshape(shape)


def _ssm_matrices(a_re, a_im, bb_re, bb_im, c_re, c_im):
    eye = jnp.eye(16, dtype=F32)

    def in_blocks(bb):
        t = bb.reshape(2, 2, 16, SSM_STATE, SSM_GROUP).transpose(0, 1, 2, 4, 3)
        return jnp.einsum('dkghp,gG->dkghGp', t, eye).reshape(2, 2, 256, 1024)

    def out_blocks(c):
        t = c.reshape(2, 2, 16, SSM_GROUP, SSM_STATE).transpose(0, 1, 2, 4, 3)
        return jnp.einsum('dkgph,gG->dkgpGh', t, eye).reshape(2, 2, 1024, 256)

    bm = jnp.concatenate([in_blocks(bb_re), in_blocks(bb_im)], axis=-1).astype(BF16)
    cm = jnp.stack([out_blocks(c_re), out_blocks(-c_im)], axis=2).astype(BF16)
    bcast = lambda a: jnp.broadcast_to(a[:, None, :], (2, SUBLANES, N_STATE))
    return bm, cm, bcast(a_re), bcast(a_im)


def _rms_modulate(x, g, sc, sh):
    ms = jnp.mean(x * x, axis=-1, keepdims=True)
    xn = x * lax.rsqrt(ms + NORM_EPS) * g
    return xn * (1.0 + sc) + sh


def _proj_body(*refs, n_norm, rope):
    if rope:
        (x_ref, sh_ref, sc_ref, g_ref, wm_ref, wvt_ref, mavg_ref, gn_ref, cos_ref, sin_ref,
         q_ref, k_ref, vt_ref, u_ref) = refs
    else:
        (x_ref, sh_ref, sc_ref, g_ref, wm_ref, wvt_ref, mavg_ref, gn_ref,
         k_ref, vt_ref, u_ref) = refs
    h = _rms_modulate(x_ref[...], g_ref[...], sc_ref[...], sh_ref[...])
    hb = h.astype(BF16)
    p = jnp.dot(hb, wm_ref[...], preferred_element_type=F32)
    qk = p[:, :n_norm]
    u_ref[...] = p[:, n_norm:]
    msq = jnp.dot((qk * qk).astype(BF16), mavg_ref[...], preferred_element_type=F32)
    qkn = qk * lax.rsqrt(msq + NORM_EPS) * gn_ref[...]
    if rope:
        rows = qkn.shape[0]
        lane = lax.broadcasted_iota(jnp.int32, (rows, LANES), 1)
        first_half = (lane % ROPE_AXIS_DIM) < (ROPE_AXIS_DIM // 2)
        cos = cos_ref[...]
        sin = sin_ref[...]
        half = ROPE_AXIS_DIM // 2
        chunks = []
        for c in range(n_norm // LANES):
            xc = qkn[:, LANES * c:LANES * (c + 1)]
            partner = jnp.where(first_half, pltpu.roll(xc, LANES - half, 1), pltpu.roll(xc, half, 1))
            chunks.append(xc * cos + partner * sin)
        qkn = jnp.concatenate(chunks, axis=1)
        q_ref[...] = qkn[:, :D_MODEL].astype(BF16)
        k_ref[...] = qkn[:, D_MODEL:].astype(BF16)
    else:
        k_ref[...] = qkn.astype(BF16)
    vt = lax.dot_general(wvt_ref[...], hb, (((1,), (1,)), ((), ())), preferred_element_type=F32)
    vt_ref[...] = vt.astype(BF16)


def _project(x, sh, sc, g, wm, wvt, mavg, gn, cos, sin):
    bsz, n, _ = x.shape
    tm = PROJ_ROWS
    rope = cos is not None
    n_norm = gn.shape[1]
    ncol = wm.shape[1]
    kdup = 2 * N_KV_HEADS * HEAD_DIM
    kv_w = N_KV_HEADS * HEAD_DIM
    mod_rows = sh.shape[0]
    mod_spec = pl.BlockSpec((None, 1, D_MODEL),
                            (lambda i, b: (b, 0, 0)) if mod_rows == bsz else (lambda i, b: (0, 0, 0)))
    in_specs = [pl.BlockSpec((None, tm, D_MODEL), lambda i, b: (b, i, 0)),
                mod_spec, mod_spec,
                _const_spec((1, D_MODEL)),
                _const_spec((D_MODEL, ncol)),
                _const_spec((kv_w, D_MODEL)),
                _const_spec((n_norm, n_norm)),
                _const_spec((1, n_norm))]
    args = [x, sh, sc, g, wm, wvt, mavg, gn]
    out_specs = [pl.BlockSpec((None, tm, kdup), lambda i, b: (b, i, 0)),
                 pl.BlockSpec((None, kv_w, tm), lambda i, b: (b, 0, i)),
                 pl.BlockSpec((tm, D_SSM), lambda i, b: (i, b))]
    out_shape = [jax.ShapeDtypeStruct((bsz, n, kdup), BF16),
                 jax.ShapeDtypeStruct((bsz, kv_w, n), BF16),
                 jax.ShapeDtypeStruct((n, bsz * D_SSM), F32)]
    if rope:
        in_specs += [pl.BlockSpec((tm, LANES), lambda i, b: (i, 0))] * 2
        args += [cos, sin]
        out_specs = [pl.BlockSpec((None, tm, D_MODEL), lambda i, b: (b, i, 0))] + out_specs
        out_shape = [jax.ShapeDtypeStruct((bsz, n, D_MODEL), BF16)] + out_shape
    outs = pl.pallas_call(
        functools.partial(_proj_body, n_norm=n_norm, rope=rope),
        grid=(n // tm, bsz),
        in_specs=in_specs,
        out_specs=out_specs,
        out_shape=out_shape,
        compiler_params=_params(2),
        name="proj_latent" if rope else "proj_ctx",
    )(*args)
    return outs if rope else [None] + list(outs)


def _attn_body(q_ref, kl_ref, kc_ref, vtl_ref, vtc_ref, o_ref):
    q = q_ref[...]
    tq = q.shape[0]
    lane = lax.broadcasted_iota(jnp.int32, (tq, LANES), 1)
    low = lane < HEAD_DIM
    outs = []
    for g in range(N_KV_HEADS):
        kg = jnp.concatenate([kc_ref[:, LANES * g:LANES * (g + 1)],
                              kl_ref[:, LANES * g:LANES * (g + 1)]], axis=0)
        vg = jnp.concatenate([vtc_ref[HEAD_DIM * g:HEAD_DIM * (g + 1), :],
                              vtl_ref[HEAD_DIM * g:HEAD_DIM * (g + 1), :]], axis=1)
        for r in range(KV_REP):
            h = KV_REP * g + r
            qc = q[:, LANES * (h // 2):LANES * (h // 2 + 1)]
            qm = jnp.where(low if h % 2 == 0 else jnp.logical_not(low), qc, jnp.zeros_like(qc))
            s_t = lax.dot_general(kg, qm, (((1,), (1,)), ((), ())), preferred_element_type=F32)
            m = jnp.max(s_t, axis=0, keepdims=True)
            p = jnp.exp(s_t - m)
            denom = jnp.sum(p, axis=0, keepdims=True)
            o_t = jnp.dot(vg, p.astype(BF16), preferred_element_type=F32)
            outs.append(o_t / denom)
    o_ref[...] = jnp.concatenate(outs, axis=0).T.astype(BF16)


def _attention(q, k_l, k_c, vt_l, vt_c):
    bsz, n, _ = q.shape
    n_ctx = k_c.shape[1]
    tq = ATTN_ROWS
    kdup = k_l.shape[2]
    kv_w = vt_l.shape[1]
    return pl.pallas_call(
        _attn_body,
        grid=(bsz, n // tq),
        in_specs=[pl.BlockSpec((None, tq, D_MODEL), lambda b, j: (b, j, 0)),
                  pl.BlockSpec((None, n, kdup), lambda b, j: (b, 0, 0)),
                  pl.BlockSpec((None, n_ctx, kdup), lambda b, j: (b, 0, 0)),
                  pl.BlockSpec((None, kv_w, n), lambda b, j: (b, 0, 0)),
                  pl.BlockSpec((None, kv_w, n_ctx), lambda b, j: (b, 0, 0))],
        out_specs=pl.BlockSpec((None, tq, D_MODEL), lambda b, j: (b, j, 0)),
        out_shape=jax.ShapeDtypeStruct((bsz, n, D_MODEL), BF16),
        compiler_params=_params(2),
        name="attn",
    )(q, k_l, k_c, vt_l, vt_c)


def _ssm_body(uc_ref, ul_ref, bm_ref, cm_ref, ar_ref, ai_ref, y_ref, bu_scr, h_scr, hr_scr, hi_scr,
              *, n_ctx_tiles):
    d = pl.program_id(0)
    i = pl.program_id(2)
    steps = uc_ref.shape[0]
    rows = steps * SSM_BATCH
    half_w = N_STATE // 2

    @pl.when(i == 0)
    def _():
        hr_scr[...] = jnp.zeros_like(hr_scr)
        hi_scr[...] = jnp.zeros_like(hi_scr)

    u = jnp.where(i < n_ctx_tiles, uc_ref[...], ul_ref[...])
    ub = u.reshape(rows, D_SSM).astype(BF16)
    for k in range(2):
        r = jnp.dot(ub[:, 256 * k:256 * (k + 1)], bm_ref[k], preferred_element_type=F32)
        bu_scr[:, half_w * k:half_w * (k + 1)] = r[:, :half_w]
        bu_scr[:, N_STATE + half_w * k:N_STATE + half_w * (k + 1)] = r[:, half_w:]

    def scan(forward):
        pairs = steps // 2
        for c in range(N_STATE // SCAN_LANES):
            re = slice(SCAN_LANES * c, SCAN_LANES * (c + 1))
            im = slice(N_STATE + SCAN_LANES * c, N_STATE + SCAN_LANES * (c + 1))
            a_r = ar_ref[:, re]
            a_i = ai_ref[:, re]

            def advance(hr, hi, row):
                br = bu_scr[pl.ds(row, SSM_BATCH), re]
                bi = bu_scr[pl.ds(row, SSM_BATCH), im]
                return a_r * hr - a_i * hi + br, a_r * hi + a_i * hr + bi

            def step(j, carry):
                hr, hi = carry
                jj = j if forward else pairs - 1 - j
                r0 = pl.multiple_of(jj * 2 * SSM_BATCH, 2 * SSM_BATCH)
                r1 = pl.multiple_of(r0 + SSM_BATCH, SSM_BATCH)
                first, second = (r0, r1) if forward else (r1, r0)
                hr1, hi1 = advance(hr, hi, first)
                hr2, hi2 = advance(hr1, hi1, second)
                top_r, bot_r = (hr1, hr2) if forward else (hr2, hr1)
                top_i, bot_i = (hi1, hi2) if forward else (hi2, hi1)
                h_scr[pl.ds(r0, 2 * SSM_BATCH), re] = jnp.concatenate([top_r, bot_r], axis=0).astype(BF16)
                h_scr[pl.ds(r0, 2 * SSM_BATCH), im] = jnp.concatenate([top_i, bot_i], axis=0).astype(BF16)
                return hr2, hi2

            hr, hi = lax.fori_loop(0, pairs, step, (hr_scr[:, re], hi_scr[:, re]))
            hr_scr[:, re] = hr
            hi_scr[:, re] = hi

    @pl.when(d == 0)
    def _():
        scan(True)

    @pl.when(d == 1)
    def _():
        scan(False)

    @pl.when(i >= n_ctx_tiles)
    def _():
        for k in range(2):
            y = (jnp.dot(h_scr[:, half_w * k:half_w * (k + 1)], cm_ref[k, 0], preferred_element_type=F32)
                 + jnp.dot(h_scr[:, N_STATE + half_w * k:N_STATE + half_w * (k + 1)], cm_ref[k, 1],
                           preferred_element_type=F32))
            y_ref[:, :, 256 * k:256 * (k + 1)] = y.reshape(steps, SSM_BATCH, 256)


def _ssm(u_c, u_l, bm, cm, a_re, a_im):
    n_ctx, bsz, _ = u_c.shape
    n = u_l.shape[0]
    t = SSM_STEPS
    nct = n_ctx // t
    nlt = n // t
    rows = t * SSM_BATCH

    def ctx_tile(d, i):
        return jnp.where(d == 0, jnp.minimum(i, nct - 1), jnp.maximum(nct - 1 - i, 0))

    def lat_tile(d, i):
        j = jnp.maximum(i - nct, 0)
        return jnp.where(d == 0, j, nlt - 1 - j)

    return pl.pallas_call(
        functools.partial(_ssm_body, n_ctx_tiles=nct),
        grid=(2, bsz // SSM_BATCH, nct + nlt),
        in_specs=[pl.BlockSpec((t, SSM_BATCH, D_SSM), lambda d, bh, i: (ctx_tile(d, i), bh, 0)),
                  pl.BlockSpec((t, SSM_BATCH, D_SSM), lambda d, bh, i: (lat_tile(d, i), bh, 0)),
                  pl.BlockSpec((None, 2, 256, N_STATE), lambda d, bh, i: (d, 0, 0, 0)),
                  pl.BlockSpec((None, 2, 2, N_STATE // 2, 256), lambda d, bh, i: (d, 0, 0, 0, 0)),
                  pl.BlockSpec((None, SUBLANES, N_STATE), lambda d, bh, i: (d, 0, 0)),
                  pl.BlockSpec((None, SUBLANES, N_STATE), lambda d, bh, i: (d, 0, 0))],
        out_specs=pl.BlockSpec((None, t, SSM_BATCH, D_SSM), lambda d, bh, i: (d, lat_tile(d, i), bh, 0)),
        out_shape=jax.ShapeDtypeStruct((2, n, bsz, D_SSM), F32),
        scratch_shapes=[pltpu.VMEM((rows, 2 * N_STATE), F32),
                        pltpu.VMEM((rows, 2 * N_STATE), BF16),
                        pltpu.VMEM((SSM_BATCH, N_STATE), F32),
                        pltpu.VMEM((SSM_BATCH, N_STATE), F32)],
        compiler_params=_params(3),
        name="ssm",
    )(u_c, u_l, bm, cm, a_re, a_im)


def _merge_body(x_ref, yf_ref, yb_ref, u_ref, at_ref,
                sh1_ref, sc1_ref, g1_ref, sh2_ref, sc2_ref, g2_ref, n1_ref, n2_ref,
                wg_ref, dv_ref, wglu_ref, bglu_ref, wba_ref, wbs_ref, wo_ref,
                w1_ref, b1_ref, w2_ref, b2_ref, o_ref):
    x = x_ref[...]
    hb = _rms_modulate(x, n1_ref[...], sc1_ref[...], sh1_ref[...]).astype(BF16)
    gates = jnp.dot(hb, wg_ref[...], preferred_element_type=F32)
    gate_a = jax.nn.sigmoid(gates[:, :D_MODEL])
    gate_s = jax.nn.sigmoid(gates[:, D_MODEL:])

    y = yf_ref[...] + yb_ref[...] + dv_ref[...] * u_ref[...]
    y = jax.nn.gelu(y)
    z = jnp.dot(y.astype(BF16), wglu_ref[...], preferred_element_type=F32) + bglu_ref[...]
    y = y * jax.nn.sigmoid(z)

    merged = (gate_a * jnp.dot(at_ref[...], wba_ref[...], preferred_element_type=F32)
              + gate_s * jnp.dot(y.astype(BF16), wbs_ref[...], preferred_element_type=F32))
    mix = jnp.dot(merged.astype(BF16), wo_ref[...], preferred_element_type=F32)
    x1 = x + g1_ref[...] * mix

    h2 = _rms_modulate(x1, n2_ref[...], sc2_ref[...], sh2_ref[...]).astype(BF16)
    acc = jnp.zeros_like(x1)
    for c in range(D_FF // FF_CHUNK):
        cols = slice(FF_CHUNK * c, FF_CHUNK * (c + 1))
        hid = jnp.dot(h2, w1_ref[:, cols], preferred_element_type=F32) + b1_ref[:, cols]
        hid = jnp.square(jnp.maximum(hid, 0.0))
        acc = acc + jnp.dot(hid.astype(BF16), w2_ref[cols, :], preferred_element_type=F32)
    o_ref[...] = x1 + g2_ref[...] * (acc + b2_ref[...])


def _merge(x, y2, u2, attn, mods, n1, n2, wg, dvec, wglu, bglu, wba, wbs, wo, w1, b1, w2, b2):
    bsz, n, _ = x.shape
    tm = MERGE_ROWS
    row_spec = pl.BlockSpec((None, tm, D_MODEL), lambda b, i: (b, i, 0))
    mod_spec = pl.BlockSpec((None, 1, D_MODEL), lambda b, i: (b, 0, 0))
    in_specs = [row_spec,
                pl.BlockSpec((None, tm, D_SSM), lambda b, i: (0, i, b)),
                pl.BlockSpec((None, tm, D_SSM), lambda b, i: (1, i, b)),
                pl.BlockSpec((tm, D_SSM), lambda b, i: (i, b)),
                row_spec] + [mod_spec] * 6
    consts = [n1, n2, wg, dvec, wglu, bglu, wba, wbs, wo, w1, b1, w2, b2]
    in_specs += [_const_spec(a.shape) for a in consts]
    return pl.pallas_call(
        _merge_body,
        grid=(bsz, n // tm),
        in_specs=in_specs,
        out_specs=row_spec,
        out_shape=jax.ShapeDtypeStruct(x.shape, x.dtype),
        compiler_params=_params(2),
        name="merge",
    )(x, y2, y2, u2, attn, *mods, *consts)


def _rope_tables(n_tok):
    half = ROPE_AXIS_DIM // 2
    pos = jnp.arange(n_tok, dtype=jnp.int32)
    row = (pos // GRID_W).astype(F32)
    col = (pos % GRID_W).astype(F32)
    inv_freq = ROPE_THETA ** (-jnp.arange(half, dtype=F32) / half)
    ang_r = row[:, None] * inv_freq
    ang_c = col[:, None] * inv_freq
    cos = jnp.concatenate([jnp.cos(ang_r)] * 2 + [jnp.cos(ang_c)] * 2, axis=-1)
    sin = jnp.concatenate([-jnp.sin(ang_r), jnp.sin(ang_r), -jnp.sin(ang_c), jnp.sin(ang_c)], axis=-1)
    reps = LANES // HEAD_DIM
    return jnp.tile(cos, (1, reps)), jnp.tile(sin, (1, reps))


def kernel(x, c, ctx, c_ctx, w_mod, b_mod, norm1_g, norm2_g, w_in, q_norm_g, k_norm_g,
           ssm_lambda_re, ssm_lambda_im, ssm_log_dt, ssm_b_re, ssm_b_im, ssm_c_re, ssm_c_im,
           ssm_d, w_glu, b_glu, w_br_attn, w_br_ssm, w_out, w_mlp1, b_mlp1, w_mlp2, b_mlp2):
    bsz, n_tok, d_model = x.shape
    n_ctx = ctx.shape[1]
    assert d_model == D_MODEL and w_mod.shape[0] == 1, "single-layer block of width 1024 only"
    assert n_tok % PROJ_ROWS == 0 and n_ctx % PROJ_ROWS == 0 and n_ctx % SSM_STEPS == 0
    assert bsz % SSM_BATCH == 0 and n_tok % GRID_W == 0

    pad = (-(bsz + 1)) % SUBLANES
    c_rows = jnp.concatenate([c, c_ctx[None, :], jnp.zeros((pad, D_MODEL), F32)], axis=0)
    mod = _modulation(c_rows, w_mod[0], b_mod[0][None, :])
    sh1, sc1, g1, sh2, sc2, g2 = [m[:, None, :] for m in jnp.split(mod, 6, axis=-1)]
    lat = lambda m: m[:bsz]
    cx = lambda m: m[bsz:bsz + 1]

    w = w_in[0]
    w_k = w[:, OFF_K:OFF_V].reshape(D_MODEL, N_KV_HEADS, 1, HEAD_DIM)
    w_kdup = jnp.broadcast_to(w_k, (D_MODEL, N_KV_HEADS, 2, HEAD_DIM)).reshape(D_MODEL, 2 * OFF_V - 2 * OFF_K)
    w_u = w[:, OFF_U:OFF_GA]
    wm_lat = jnp.concatenate([w[:, :OFF_K], w_kdup, w_u], axis=1).astype(BF16)
    wm_ctx = jnp.concatenate([w_kdup, w_u], axis=1).astype(BF16)
    w_vt = w[:, OFF_V:OFF_U].T.astype(BF16)
    w_gate = w[:, OFF_GA:].astype(BF16)
    n_lat = D_MODEL + w_kdup.shape[1]
    mavg = jnp.kron(jnp.eye(n_lat // HEAD_DIM, dtype=F32),
                    jnp.full((HEAD_DIM, HEAD_DIM), 1.0 / HEAD_DIM, F32)).astype(BF16)
    gk = jnp.tile(k_norm_g[0], 2 * N_KV_HEADS)
    gn_lat = jnp.concatenate([jnp.tile(q_norm_g[0], N_HEADS) * HEAD_DIM ** -0.5, gk])[None, :]
    gn_ctx = gk[None, :]
    cos, sin = _rope_tables(n_tok)
    n1 = norm1_g[0][None, :]

    q, k_l, vt_l, u_l = _project(x, lat(sh1), lat(sc1), n1, wm_lat, w_vt, mavg, gn_lat, cos, sin)
    _, k_c, vt_c, u_c = _project(ctx, cx(sh1), cx(sc1), n1, wm_ctx, w_vt,
                                 mavg[:gk.shape[0], :gk.shape[0]], gn_ctx, None, None)

    attn = _attention(q, k_l, k_c, vt_l, vt_c)

    a_re, a_im, bb_re, bb_im = _discretize(ssm_lambda_re[0], ssm_lambda_im[0], ssm_log_dt[0],
                                           ssm_b_re[0], ssm_b_im[0])
    bm, cm, a_re8, a_im8 = _ssm_matrices(a_re, a_im, bb_re, bb_im, ssm_c_re[0], ssm_c_im[0])
    y = _ssm(u_c.reshape(n_ctx, bsz, D_SSM), u_l.reshape(n_tok, bsz, D_SSM), bm, cm, a_re8, a_im8)

    mods = [lat(m) for m in (sh1, sc1, g1, sh2, sc2, g2)]
    return _merge(x, y.reshape(2, n_tok, bsz * D_SSM), u_l, attn, mods, n1, norm2_g[0][None, :],
                  w_gate, ssm_d[0][None, :], w_glu[0].astype(BF16), b_glu[0][None, :],
                  w_br_attn[0].astype(BF16), w_br_ssm[0].astype(BF16), w_out[0].astype(BF16),
                  w_mlp1[0].astype(BF16), b_mlp1[0][None, :], w_mlp2[0].astype(BF16), b_mlp2[0][None, :])
```

```python
import functools

import jax
import jax.numpy as jnp
from jax import lax
from jax.experimental import pallas as pl
from jax.experimental.pallas import tpu as pltpu

F32 = jnp.float32
BF16 = jnp.bfloat16

D_MODEL = 1024
N_HEADS = 16
N_KV_HEADS = 4
HEAD_DIM = 64
KV_REP = N_HEADS // N_KV_HEADS
GRID_W = 64
ROPE_THETA = 10000.0
ROPE_AXIS_DIM = HEAD_DIM // 2
D_SSM = 512
SSM_GROUP = 16
N_SSM_GROUPS = 32
SSM_STATE = 64
N_STATE = N_SSM_GROUPS * SSM_STATE
D_FF = 4 * D_MODEL
NORM_EPS = 1e-6
OFF_K = 1024
OFF_V = 1280
OFF_U = 1536
OFF_GA = 2048

LANES = 128
SUBLANES = 8
MXU_TILE = 256
LOG2_E = 1.4426950408889634
VMEM_LIMIT_BYTES = 56 * 1024 * 1024

PROJ_ROWS = 256
ATTN_ROWS = 256
MERGE_ROWS = 256
SSM_STEPS = 64
SSM_BATCH = SUBLANES
SCAN_LANES = 1024
FF_CHUNK = 1024


def _const_spec(shape):
    nd = len(shape)
    return pl.BlockSpec(shape, lambda *_: (0,) * nd, pipeline_mode=pl.Buffered(1))


def _params(n_axes, vmem=VMEM_LIMIT_BYTES):
    return pltpu.CompilerParams(dimension_semantics=("arbitrary",) * n_axes, vmem_limit_bytes=vmem)


def _mod_body(c_ref, w_ref, b_ref, o_ref):
    c = c_ref[...]
    s = c * jax.nn.sigmoid(c)
    o_ref[...] = jnp.dot(s, w_ref[...], preferred_element_type=F32) + b_ref[...]


def _modulation(c_rows, w_mod, b_mod):
    rows = c_rows.shape[0]
    ncol = w_mod.shape[1]
    blk = D_MODEL
    return pl.pallas_call(
        _mod_body,
        grid=(ncol // blk,),
        in_specs=[pl.BlockSpec((rows, D_MODEL), lambda j: (0, 0)),
                  pl.BlockSpec((D_MODEL, blk), lambda j: (0, j)),
                  pl.BlockSpec((1, blk), lambda j: (0, j))],
        out_specs=pl.BlockSpec((rows, blk), lambda j: (0, j)),
        out_shape=jax.ShapeDtypeStruct((rows, ncol), F32),
        compiler_params=_params(1),
        name="mod",
    )(c_rows, w_mod, b_mod)


def _disc_body(lr_ref, li_ref, ldt_ref, br_ref, bi_ref, ar_ref, ai_ref, bbr_ref, bbi_ref):
    lr = lr_ref[...]
    li = li_ref[...]
    dt = jnp.exp(ldt_ref[...])
    mag = jnp.exp(lr * dt)
    ar = mag * jnp.cos(li * dt)
    ai = mag * jnp.sin(li * dt)
    nr = ar - 1.0
    den = lr * lr + li * li
    cr = (nr * lr + ai * li) / den
    ci = (ai * lr - nr * li) / den
    br = br_ref[...]
    bi = bi_ref[...]
    ar_ref[...] = ar
    ai_ref[...] = ai
    bbr_ref[...] = cr * br - ci * bi
    bbi_ref[...] = cr * bi + ci * br


def _discretize(lam_re, lam_im, log_dt, b_re, b_im):
    rows = 2 * N_SSM_GROUPS
    cols = SSM_STATE * SSM_GROUP
    rep = lambda a: jnp.repeat(a.reshape(rows, SSM_STATE), SSM_GROUP, axis=1)
    ldt = jnp.broadcast_to(log_dt.reshape(rows, 1), (rows, cols))
    spec = pl.BlockSpec((rows, cols), lambda: (0, 0))
    sds = jax.ShapeDtypeStruct((rows, cols), F32)
    ar, ai, bbr, bbi = pl.pallas_call(
        _disc_body,
        in_specs=[spec] * 5,
        out_specs=[spec] * 4,
        out_shape=[sds] * 4,
        name="disc",
    )(rep(lam_re), rep(lam_im), ldt, b_re.reshape(rows, cols), b_im.reshape(rows, cols))
    a_re = ar[:, ::SSM_GROUP].reshape(2, N_STATE)
    a_im = ai[:, ::SSM_GROUP].reshape(2, N_STATE)
    shape = (2, N_SSM_GROUPS, SSM_STATE, SSM_GROUP)
    return a_re, a_im, bbr.reshape(shape), bbi.reshape(shape)


def _ssm_matrices(a_re, a_im, bb_re, bb_im, c_re, c_im):
    eye = jnp.eye(16, dtype=F32)

    def in_blocks(bb):
        t = bb.reshape(2, 2, 16, SSM_STATE, SSM_GROUP).transpose(0, 1, 2, 4, 3)
        return jnp.einsum('dkghp,gG->dkghGp', t, eye).reshape(2, 2, 256, 1024)

    def out_blocks(c):
        t = c.reshape(2, 2, 16, SSM_GROUP, SSM_STATE).transpose(0, 1, 2, 4, 3)
        return jnp.einsum('dkgph,gG->dkgpGh', t, eye).reshape(2, 2, 1024, 256)

    bm = jnp.concatenate([in_blocks(bb_re), in_blocks(bb_im)], axis=-1).astype(BF16)
    cm = jnp.stack([out_blocks(c_re), out_blocks(-c_im)], axis=2).astype(BF16)
    bcast = lambda a: jnp.broadcast_to(a[:, None, :], (2, SUBLANES, N_STATE))
    return bm, cm, bcast(a_re), bcast(a_im)


def _rms_modulate(x, g, sc, sh):
    ms = jnp.mean(x * x, axis=-1, keepdims=True)
    xn = x * lax.rsqrt(ms + NORM_EPS) * g
    return xn * (1.0 + sc) + sh


def _proj_body(*refs, n_norm, rope):
    if rope:
        (x_ref, sh_ref, sc_ref, g_ref, wm_ref, wvt_ref, mavg_ref, gn_ref, cos_ref, sin_ref,
         q_ref, k_ref, vt_ref, u_ref) = refs
    else:
        (x_ref, sh_ref, sc_ref, g_ref, wm_ref, wvt_ref, mavg_ref, gn_ref,
         k_ref, vt_ref, u_ref) = refs
    h = _rms_modulate(x_ref[...], g_ref[...], sc_ref[...], sh_ref[...])
    hb = h.astype(BF16)
    p = jnp.dot(hb, wm_ref[...], preferred_element_type=F32)
    qk = p[:, :n_norm]
    u_ref[...] = p[:, n_norm:]
    sq = (qk * qk).astype(BF16)
    mavg = mavg_ref[...]
    width = mavg.shape[0]
    msq = jnp.concatenate(
        [jnp.dot(sq[:, width * j:width * (j + 1)], mavg, preferred_element_type=F32)
         for j in range(n_norm // width)], axis=1)
    qkn = qk * lax.rsqrt(msq + NORM_EPS) * gn_ref[...]
    n_q_chunks = n_norm // LANES - k_ref.shape[0]
    if rope:
        lane = lax.broadcasted_iota(jnp.int32, (qkn.shape[0], LANES), 1)
        half = ROPE_AXIS_DIM // 2
        first_half = (lane % ROPE_AXIS_DIM) < half
        cos = cos_ref[...]
        sin = sin_ref[...]
    for c in range(n_norm // LANES):
        xc = qkn[:, LANES * c:LANES * (c + 1)]
        if rope:
            partner = jnp.where(first_half, pltpu.roll(xc, LANES - half, 1), pltpu.roll(xc, half, 1))
            xc = xc * cos + partner * sin
        if c < n_q_chunks:
            q_ref[c] = xc.astype(BF16)
        else:
            k_ref[c - n_q_chunks] = xc.astype(BF16)
    vt = lax.dot_general(wvt_ref[...], hb, (((1,), (1,)), ((), ())), preferred_element_type=F32)
    for g in range(N_KV_HEADS):
        vt_ref[g] = vt[HEAD_DIM * g:HEAD_DIM * (g + 1), :].astype(BF16)


def _project(x, sh, sc, g, wm, wvt, mavg, gn, cos, sin):
    bsz, n, _ = x.shape
    tm = PROJ_ROWS
    rope = cos is not None
    n_norm = gn.shape[1]
    ncol = wm.shape[1]
    kv_w = N_KV_HEADS * HEAD_DIM
    q_chunks = D_MODEL // LANES
    mod_rows = sh.shape[0]
    mod_spec = pl.BlockSpec((None, 1, D_MODEL),
                            (lambda i, b: (b, 0, 0)) if mod_rows == bsz else (lambda i, b: (0, 0, 0)))
    in_specs = [pl.BlockSpec((None, tm, D_MODEL), lambda i, b: (b, i, 0)),
                mod_spec, mod_spec,
                _const_spec((1, D_MODEL)),
                _const_spec((D_MODEL, ncol)),
                _const_spec((kv_w, D_MODEL)),
                _const_spec(mavg.shape),
                _const_spec((1, n_norm))]
    args = [x, sh, sc, g, wm, wvt, mavg, gn]
    out_specs = [pl.BlockSpec((None, N_KV_HEADS, tm, LANES), lambda i, b: (b, 0, i, 0)),
                 pl.BlockSpec((None, N_KV_HEADS, HEAD_DIM, tm), lambda i, b: (b, 0, 0, i)),
                 pl.BlockSpec((tm, D_SSM), lambda i, b: (i, b))]
    out_shape = [jax.ShapeDtypeStruct((bsz, N_KV_HEADS, n, LANES), BF16),
                 jax.ShapeDtypeStruct((bsz, N_KV_HEADS, HEAD_DIM, n), BF16),
                 jax.ShapeDtypeStruct((n, bsz * D_SSM), F32)]
    if rope:
        in_specs += [pl.BlockSpec((tm, LANES), lambda i, b: (i, 0))] * 2
        args += [cos, sin]
        out_specs = [pl.BlockSpec((None, q_chunks, tm, LANES), lambda i, b: (b, 0, i, 0))] + out_specs
        out_shape = [jax.ShapeDtypeStruct((bsz, q_chunks, n, LANES), BF16)] + out_shape
    outs = pl.pallas_call(
        functools.partial(_proj_body, n_norm=n_norm, rope=rope),
        grid=(n // tm, bsz),
        in_specs=in_specs,
        out_specs=out_specs,
        out_shape=out_shape,
        compiler_params=_params(2),
        name="proj_latent" if rope else "proj_ctx",
    )(*args)
    return outs if rope else [None] + list(outs)


def _attn_body(q_ref, kl_ref, kc_ref, vtl_ref, vtc_ref, o_ref, s0_scr, s1_scr, o_scr):
    n_ctx = kc_ref.shape[1]
    tq = q_ref.shape[1]
    lane = lax.broadcasted_iota(jnp.int32, (tq, LANES), 1)
    head_lanes = (lane < HEAD_DIM, lane >= HEAD_DIM)
    s_bufs = (s0_scr, s1_scr)
    nt = (((1,), (1,)), ((), ()))

    def scores(g, r, s_scr):
        qc = q_ref[2 * g + r // 2]
        qm = jnp.where(head_lanes[r % 2], qc, jnp.zeros_like(qc))
        s_scr[:n_ctx, :] = lax.dot_general(kc_ref[g], qm, nt, preferred_element_type=F32)
        s_scr[n_ctx:, :] = lax.dot_general(kl_ref[g], qm, nt, preferred_element_type=F32)

    def finish(g, r, s_scr):
        s_t = s_scr[...]
        m = jnp.max(s_t, axis=0, keepdims=True)
        p = jnp.exp2(s_t - m)
        denom = jnp.sum(p, axis=0, keepdims=True)
        pb = p.astype(BF16)
        o_t = (jnp.dot(vtc_ref[g], pb[:n_ctx], preferred_element_type=F32)
               + jnp.dot(vtl_ref[g], pb[n_ctx:], preferred_element_type=F32))
        o_scr[KV_REP * g + r] = o_t / denom

    scores(0, 0, s_bufs[0])

    def group(g, carry):
        for r in range(KV_REP):
            if r + 1 < KV_REP:
                scores(g, r + 1, s_bufs[(r + 1) % 2])
            else:
                scores(jnp.minimum(g + 1, N_KV_HEADS - 1), 0, s_bufs[(r + 1) % 2])
            finish(g, r, s_bufs[r % 2])
        return carry

    lax.fori_loop(0, N_KV_HEADS, group, 0)
    o_ref[...] = o_scr[...].reshape(N_HEADS * HEAD_DIM, tq).T.astype(BF16)


def _attention(q, k_l, k_c, vt_l, vt_c):
    bsz, q_chunks, n, _ = q.shape
    n_ctx = k_c.shape[2]
    tq = ATTN_ROWS
    whole = lambda shape: pl.BlockSpec((None,) + shape, lambda b, j: (b,) + (0,) * len(shape))
    return pl.pallas_call(
        _attn_body,
        grid=(bsz, n // tq),
        in_specs=[pl.BlockSpec((None, q_chunks, tq, LANES), lambda b, j: (b, 0, j, 0)),
                  whole((N_KV_HEADS, n, LANES)),
                  whole((N_KV_HEADS, n_ctx, LANES)),
                  whole((N_KV_HEADS, HEAD_DIM, n)),
                  whole((N_KV_HEADS, HEAD_DIM, n_ctx))],
        out_specs=pl.BlockSpec((None, tq, D_MODEL), lambda b, j: (b, j, 0)),
        out_shape=jax.ShapeDtypeStruct((bsz, n, D_MODEL), BF16),
        scratch_shapes=[pltpu.VMEM((n_ctx + n, tq), F32),
                        pltpu.VMEM((n_ctx + n, tq), F32),
                        pltpu.VMEM((N_HEADS, HEAD_DIM, tq), F32)],
        compiler_params=_params(2),
        name="attn",
    )(q, k_l, k_c, vt_l, vt_c)


def _ssm_body(uc_ref, ul_ref, bm_ref, cm_ref, ar_ref, ai_ref, y_ref, bu_scr, h_scr, hr_scr, hi_scr,
              *, n_ctx_tiles):
    d = pl.program_id(0)
    i = pl.program_id(2)
    steps = uc_ref.shape[0]
    rows = steps * SSM_BATCH
    half_w = N_STATE // 2

    @pl.when(i == 0)
    def _():
        hr_scr[...] = jnp.zeros_like(hr_scr)
        hi_scr[...] = jnp.zeros_like(hi_scr)

    u = jnp.where(i < n_ctx_tiles, uc_ref[...], ul_ref[...])
    ub = u.reshape(rows, D_SSM).astype(BF16)
    for k in range(2):
        r = jnp.dot(ub[:, 256 * k:256 * (k + 1)], bm_ref[k], preferred_element_type=F32)
        bu_scr[:, half_w * k:half_w * (k + 1)] = r[:, :half_w]
        bu_scr[:, N_STATE + half_w * k:N_STATE + half_w * (k + 1)] = r[:, half_w:]

    def scan(forward):
        pairs = steps // 2
        for c in range(N_STATE // SCAN_LANES):
            re = slice(SCAN_LANES * c, SCAN_LANES * (c + 1))
            im = slice(N_STATE + SCAN_LANES * c, N_STATE + SCAN_LANES * (c + 1))
            a_r = ar_ref[:, re]
            a_i = ai_ref[:, re]

            def advance(hr, hi, row):
                br = bu_scr[pl.ds(row, SSM_BATCH), re]
                bi = bu_scr[pl.ds(row, SSM_BATCH), im]
                return a_r * hr - a_i * hi + br, a_r * hi + a_i * hr + bi

            def step(j, carry):
                hr, hi = carry
                jj = j if forward else pairs - 1 - j
                r0 = pl.multiple_of(jj * 2 * SSM_BATCH, 2 * SSM_BATCH)
                r1 = pl.multiple_of(r0 + SSM_BATCH, SSM_BATCH)
                first, second = (r0, r1) if forward else (r1, r0)
                hr1, hi1 = advance(hr, hi, first)
                hr2, hi2 = advance(hr1, hi1, second)
                top_r, bot_r = (hr1, hr2) if forward else (hr2, hr1)
                top_i, bot_i = (hi1, hi2) if forward else (hi2, hi1)
                h_scr[pl.ds(r0, 2 * SSM_BATCH), re] = jnp.concatenate([top_r, bot_r], axis=0).astype(BF16)
                h_scr[pl.ds(r0, 2 * SSM_BATCH), im] = jnp.concatenate([top_i, bot_i], axis=0).astype(BF16)
                return hr2, hi2

            hr, hi = lax.fori_loop(0, pairs, step, (hr_scr[:, re], hi_scr[:, re]))
            hr_scr[:, re] = hr
            hi_scr[:, re] = hi

    @pl.when(d == 0)
    def _():
        scan(True)

    @pl.when(d == 1)
    def _():
        scan(False)

    @pl.when(i >= n_ctx_tiles)
    def _():
        for k in range(2):
            y = (jnp.dot(h_scr[:, half_w * k:half_w * (k + 1)], cm_ref[k, 0], preferred_element_type=F32)
                 + jnp.dot(h_scr[:, N_STATE + half_w * k:N_STATE + half_w * (k + 1)], cm_ref[k, 1],
                           preferred_element_type=F32))
            y_ref[:, :, 256 * k:256 * (k + 1)] = y.reshape(steps, SSM_BATCH, 256)


def _ssm(u_c, u_l, bm, cm, a_re, a_im):
    n_ctx, bsz, _ = u_c.shape
    n = u_l.shape[0]
    t = SSM_STEPS
    nct = n_ctx // t
    nlt = n // t
    rows = t * SSM_BATCH

    def ctx_tile(d, i):
        return jnp.where(d == 0, jnp.minimum(i, nct - 1), jnp.maximum(nct - 1 - i, 0))

    def lat_tile(d, i):
        j = jnp.maximum(i - nct, 0)
        return jnp.where(d == 0, j, nlt - 1 - j)

    return pl.pallas_call(
        functools.partial(_ssm_body, n_ctx_tiles=nct),
        grid=(2, bsz // SSM_BATCH, nct + nlt),
        in_specs=[pl.BlockSpec((t, SSM_BATCH, D_SSM), lambda d, bh, i: (ctx_tile(d, i), bh, 0)),
                  pl.BlockSpec((t, SSM_BATCH, D_SSM), lambda d, bh, i: (lat_tile(d, i), bh, 0)),
                  pl.BlockSpec((None, 2, 256, N_STATE), lambda d, bh, i: (d, 0, 0, 0)),
                  pl.BlockSpec((None, 2, 2, N_STATE // 2, 256), lambda d, bh, i: (d, 0, 0, 0, 0)),
                  pl.BlockSpec((None, SUBLANES, N_STATE), lambda d, bh, i: (d, 0, 0)),
                  pl.BlockSpec((None, SUBLANES, N_STATE), lambda d, bh, i: (d, 0, 0))],
        out_specs=pl.BlockSpec((None, t, SSM_BATCH, D_SSM), lambda d, bh, i: (d, lat_tile(d, i), bh, 0)),
        out_shape=jax.ShapeDtypeStruct((2, n, bsz, D_SSM), F32),
        scratch_shapes=[pltpu.VMEM((rows, 2 * N_STATE), F32),
                        pltpu.VMEM((rows, 2 * N_STATE), BF16),
                        pltpu.VMEM((SSM_BATCH, N_STATE), F32),
                        pltpu.VMEM((SSM_BATCH, N_STATE), F32)],
        compiler_params=_params(3),
        name="ssm",
    )(u_c, u_l, bm, cm, a_re, a_im)


def _merge_body(x_ref, yf_ref, yb_ref, u_ref, at_ref,
                sh1_ref, sc1_ref, g1_ref, sh2_ref, sc2_ref, g2_ref, n1_ref, n2_ref,
                wg_ref, dv_ref, wglu_ref, bglu_ref, wba_ref, wbs_ref, wo_ref,
                w1_ref, b1_ref, w2_ref, b2_ref, o_ref):
    x = x_ref[...]
    hb = _rms_modulate(x, n1_ref[...], sc1_ref[...], sh1_ref[...]).astype(BF16)
    gates = jnp.dot(hb, wg_ref[...], preferred_element_type=F32)
    gate_a = jax.nn.sigmoid(gates[:, :D_MODEL])
    gate_s = jax.nn.sigmoid(gates[:, D_MODEL:])

    y = yf_ref[...] + yb_ref[...] + dv_ref[...] * u_ref[...]
    y = jax.nn.gelu(y)
    z = jnp.dot(y.astype(BF16), wglu_ref[...], preferred_element_type=F32) + bglu_ref[...]
    y = y * jax.nn.sigmoid(z)

    merged = (gate_a * jnp.dot(at_ref[...], wba_ref[...], preferred_element_type=F32)
              + gate_s * jnp.dot(y.astype(BF16), wbs_ref[...], preferred_element_type=F32))
    mix = jnp.dot(merged.astype(BF16), wo_ref[...], preferred_element_type=F32)
    x1 = x + g1_ref[...] * mix

    h2 = _rms_modulate(x1, n2_ref[...], sc2_ref[...], sh2_ref[...]).astype(BF16)
    acc = jnp.zeros_like(x1)
    for c in range(D_FF // FF_CHUNK):
        cols = slice(FF_CHUNK * c, FF_CHUNK * (c + 1))
        hid = jnp.dot(h2, w1_ref[:, cols], preferred_element_type=F32) + b1_ref[:, cols]
        hid = jnp.square(jnp.maximum(hid, 0.0))
        acc = acc + jnp.dot(hid.astype(BF16), w2_ref[cols, :], preferred_element_type=F32)
    o_ref[...] = x1 + g2_ref[...] * (acc + b2_ref[...])


def _merge(x, y2, u2, attn, mods, n1, n2, wg, dvec, wglu, bglu, wba, wbs, wo, w1, b1, w2, b2):
    bsz, n, _ = x.shape
    tm = MERGE_ROWS
    row_spec = pl.BlockSpec((None, tm, D_MODEL), lambda b, i: (b, i, 0))
    mod_spec = pl.BlockSpec((None, 1, D_MODEL), lambda b, i: (b, 0, 0))
    in_specs = [row_spec,
                pl.BlockSpec((None, tm, D_SSM), lambda b, i: (0, i, b)),
                pl.BlockSpec((None, tm, D_SSM), lambda b, i: (1, i, b)),
                pl.BlockSpec((tm, D_SSM), lambda b, i: (i, b)),
                row_spec] + [mod_spec] * 6
    consts = [n1, n2, wg, dvec, wglu, bglu, wba, wbs, wo, w1, b1, w2, b2]
    in_specs += [_const_spec(a.shape) for a in consts]
    return pl.pallas_call(
        _merge_body,
        grid=(bsz, n // tm),
        in_specs=in_specs,
        out_specs=row_spec,
        out_shape=jax.ShapeDtypeStruct(x.shape, x.dtype),
        compiler_params=_params(2),
        name="merge",
    )(x, y2, y2, u2, attn, *mods, *consts)


def _rope_tables(n_tok):
    half = ROPE_AXIS_DIM // 2
    pos = jnp.arange(n_tok, dtype=jnp.int32)
    row = (pos // GRID_W).astype(F32)
    col = (pos % GRID_W).astype(F32)
    inv_freq = ROPE_THETA ** (-jnp.arange(half, dtype=F32) / half)
    ang_r = row[:, None] * inv_freq
    ang_c = col[:, None] * inv_freq
    cos = jnp.concatenate([jnp.cos(ang_r)] * 2 + [jnp.cos(ang_c)] * 2, axis=-1)
    sin = jnp.concatenate([-jnp.sin(ang_r), jnp.sin(ang_r), -jnp.sin(ang_c), jnp.sin(ang_c)], axis=-1)
    reps = LANES // HEAD_DIM
    return jnp.tile(cos, (1, reps)), jnp.tile(sin, (1, reps))


def kernel(x, c, ctx, c_ctx, w_mod, b_mod, norm1_g, norm2_g, w_in, q_norm_g, k_norm_g,
           ssm_lambda_re, ssm_lambda_im, ssm_log_dt, ssm_b_re, ssm_b_im, ssm_c_re, ssm_c_im,
           ssm_d, w_glu, b_glu, w_br_attn, w_br_ssm, w_out, w_mlp1, b_mlp1, w_mlp2, b_mlp2):
    bsz, n_tok, d_model = x.shape
    n_ctx = ctx.shape[1]
    assert d_model == D_MODEL and w_mod.shape[0] == 1, "single-layer block of width 1024 only"
    assert n_tok % PROJ_ROWS == 0 and n_ctx % PROJ_ROWS == 0 and n_ctx % SSM_STEPS == 0
    assert bsz % SSM_BATCH == 0 and n_tok % GRID_W == 0

    pad = (-(bsz + 1)) % SUBLANES
    c_rows = jnp.concatenate([c, c_ctx[None, :], jnp.zeros((pad, D_MODEL), F32)], axis=0)
    mod = _modulation(c_rows, w_mod[0], b_mod[0][None, :])
    sh1, sc1, g1, sh2, sc2, g2 = [m[:, None, :] for m in jnp.split(mod, 6, axis=-1)]
    lat = lambda m: m[:bsz]
    cx = lambda m: m[bsz:bsz + 1]

    w = w_in[0]
    w_k = w[:, OFF_K:OFF_V].reshape(D_MODEL, N_KV_HEADS, 1, HEAD_DIM)
    w_kdup = jnp.broadcast_to(w_k, (D_MODEL, N_KV_HEADS, 2, HEAD_DIM)).reshape(D_MODEL, 2 * OFF_V - 2 * OFF_K)
    w_u = w[:, OFF_U:OFF_GA]
    wm_lat = jnp.concatenate([w[:, :OFF_K], w_kdup, w_u], axis=1).astype(BF16)
    wm_ctx = jnp.concatenate([w_kdup, w_u], axis=1).astype(BF16)
    w_vt = w[:, OFF_V:OFF_U].T.astype(BF16)
    w_gate = w[:, OFF_GA:].astype(BF16)
    mavg = jnp.kron(jnp.eye(MXU_TILE // HEAD_DIM, dtype=F32),
                    jnp.full((HEAD_DIM, HEAD_DIM), 1.0 / HEAD_DIM, F32)).astype(BF16)
    gk = jnp.tile(k_norm_g[0], 2 * N_KV_HEADS)
    gq = jnp.tile(q_norm_g[0], N_HEADS) * (HEAD_DIM ** -0.5 * LOG2_E)
    gn_lat = jnp.concatenate([gq, gk])[None, :]
    gn_ctx = gk[None, :]
    cos, sin = _rope_tables(n_tok)
    n1 = norm1_g[0][None, :]

    q, k_l, vt_l, u_l = _project(x, lat(sh1), lat(sc1), n1, wm_lat, w_vt, mavg, gn_lat, cos, sin)
    _, k_c, vt_c, u_c = _project(ctx, cx(sh1), cx(sc1), n1, wm_ctx, w_vt, mavg, gn_ctx, None, None)

    attn = _attention(q, k_l, k_c, vt_l, vt_c)

    a_re, a_im, bb_re, bb_im = _discretize(ssm_lambda_re[0], ssm_lambda_im[0], ssm_log_dt[0],
                                           ssm_b_re[0], ssm_b_im[0])
    bm, cm, a_re8, a_im8 = _ssm_matrices(a_re, a_im, bb_re, bb_im, ssm_c_re[0], ssm_c_im[0])
    y = _ssm(u_c.reshape(n_ctx, bsz, D_SSM), u_l.reshape(n_tok, bsz, D_SSM), bm, cm, a_re8, a_im8)

    mods = [lat(m) for m in (sh1, sc1, g1, sh2, sc2, g2)]
    return _merge(x, y.reshape(2, n_tok, bsz * D_SSM), u_l, attn, mods, n1, norm2_g[0][None, :],
                  w_gate, ssm_d[0][None, :], w_glu[0].astype(BF16), b_glu[0][None, :],
                  w_br_attn[0].astype(BF16), w_br_ssm[0].astype(BF16), w_out[0].astype(BF16),
                  w_mlp1[0].astype(BF16), b_mlp1[0][None, :], w_mlp2[0].astype(BF16), b_mlp2[0][None, :])
```

```python
import functools

import jax
import jax.numpy as jnp
from jax import lax
from jax.experimental import pallas as pl
from jax.experimental.pallas import tpu as pltpu

F32 = jnp.float32
BF16 = jnp.bfloat16

D_MODEL = 1024
N_HEADS = 16
N_KV_HEADS = 4
HEAD_DIM = 64
KV_REP = N_HEADS // N_KV_HEADS
GRID_W = 64
ROPE_THETA = 10000.0
ROPE_AXIS_DIM = HEAD_DIM // 2
D_SSM = 512
SSM_GROUP = 16
N_SSM_GROUPS = 32
SSM_STATE = 64
N_STATE = N_SSM_GROUPS * SSM_STATE
D_FF = 4 * D_MODEL
NORM_EPS = 1e-6
OFF_K = 1024
OFF_V = 1280
OFF_U = 1536
OFF_GA = 2048

LANES = 128
SUBLANES = 8
MXU_TILE = 256
LOG2_E = 1.4426950408889634
VMEM_LIMIT_BYTES = 56 * 1024 * 1024

PROJ_ROWS = 256
ATTN_ROWS = 256
ATTN_KEYS = 256
MERGE_ROWS = 256
SSM_STEPS = 64
SSM_BATCH = SUBLANES
SCAN_LANES = 1024
FF_CHUNK = 1024


def _const_spec(shape):
    nd = len(shape)
    return pl.BlockSpec(shape, lambda *_: (0,) * nd, pipeline_mode=pl.Buffered(1))


def _params(n_axes, vmem=VMEM_LIMIT_BYTES, flags=None):
    return pltpu.CompilerParams(dimension_semantics=("arbitrary",) * n_axes, vmem_limit_bytes=vmem,
                                flags=flags)


def _mod_body(c_ref, w_ref, b_ref, o_ref):
    c = c_ref[...]
    s = c * jax.nn.sigmoid(c)
    o_ref[...] = jnp.dot(s, w_ref[...], preferred_element_type=F32) + b_ref[...]


def _modulation(c_rows, w_mod, b_mod):
    rows = c_rows.shape[0]
    ncol = w_mod.shape[1]
    blk = D_MODEL
    return pl.pallas_call(
        _mod_body,
        grid=(ncol // blk,),
        in_specs=[pl.BlockSpec((rows, D_MODEL), lambda j: (0, 0)),
                  pl.BlockSpec((D_MODEL, blk), lambda j: (0, j)),
                  pl.BlockSpec((1, blk), lambda j: (0, j))],
        out_specs=pl.BlockSpec((rows, blk), lambda j: (0, j)),
        out_shape=jax.ShapeDtypeStruct((rows, ncol), F32),
        compiler_params=_params(1),
        name="mod",
    )(c_rows, w_mod, b_mod)


def _disc_body(lr_ref, li_ref, ldt_ref, br_ref, bi_ref, ar_ref, ai_ref, bbr_ref, bbi_ref):
    lr = lr_ref[...]
    li = li_ref[...]
    dt = jnp.exp(ldt_ref[...])
    mag = jnp.exp(lr * dt)
    ar = mag * jnp.cos(li * dt)
    ai = mag * jnp.sin(li * dt)
    nr = ar - 1.0
    den = lr * lr + li * li
    cr = (nr * lr + ai * li) / den
    ci = (ai * lr - nr * li) / den
    br = br_ref[...]
    bi = bi_ref[...]
    ar_ref[...] = ar
    ai_ref[...] = ai
    bbr_ref[...] = cr * br - ci * bi
    bbi_ref[...] = cr * bi + ci * br


def _discretize(lam_re, lam_im, log_dt, b_re, b_im):
    rows = 2 * N_SSM_GROUPS
    cols = SSM_STATE * SSM_GROUP
    rep = lambda a: jnp.repeat(a.reshape(rows, SSM_STATE), SSM_GROUP, axis=1)
    ldt = jnp.broadcast_to(log_dt.reshape(rows, 1), (rows, cols))
    spec = pl.BlockSpec((rows, cols), lambda: (0, 0))
    sds = jax.ShapeDtypeStruct((rows, cols), F32)
    ar, ai, bbr, bbi = pl.pallas_call(
        _disc_body,
        in_specs=[spec] * 5,
        out_specs=[spec] * 4,
        out_shape=[sds] * 4,
        name="disc",
    )(rep(lam_re), rep(lam_im), ldt, b_re.reshape(rows, cols), b_im.reshape(rows, cols))
    a_re = ar[:, ::SSM_GROUP].reshape(2, N_STATE)
    a_im = ai[:, ::SSM_GROUP].reshape(2, N_STATE)
    shape = (2, N_SSM_GROUPS, SSM_STATE, SSM_GROUP)
    return a_re, a_im, bbr.reshape(shape), bbi.reshape(shape)


def _ssm_matrices(a_re, a_im, bb_re, bb_im, c_re, c_im):
    eye = jnp.eye(16, dtype=F32)

    def in_blocks(bb):
        t = bb.reshape(2, 2, 16, SSM_STATE, SSM_GROUP).transpose(0, 1, 2, 4, 3)
        return jnp.einsum('dkghp,gG->dkghGp', t, eye).reshape(2, 2, 256, 1024)

    def out_blocks(c):
        t = c.reshape(2, 2, 16, SSM_GROUP, SSM_STATE).transpose(0, 1, 2, 4, 3)
        return jnp.einsum('dkgph,gG->dkgpGh', t, eye).reshape(2, 2, 1024, 256)

    bm = jnp.concatenate([in_blocks(bb_re), in_blocks(bb_im)], axis=-1).astype(BF16)
    cm = jnp.stack([out_blocks(c_re), out_blocks(-c_im)], axis=2).astype(BF16)
    bcast = lambda a: jnp.broadcast_to(a[:, None, :], (2, SUBLANES, N_STATE))
    return bm, cm, bcast(a_re), bcast(a_im)


def _rms_modulate(x, g, sc, sh):
    ms = jnp.mean(x * x, axis=-1, keepdims=True)
    xn = x * lax.rsqrt(ms + NORM_EPS) * g
    return xn * (1.0 + sc) + sh


def _proj_body(*refs, n_norm, rope):
    if rope:
        (x_ref, sh_ref, sc_ref, g_ref, wm_ref, wvt_ref, mavg_ref, gn_ref, cos_ref, sin_ref,
         q_ref, k_ref, vt_ref, u_ref) = refs
    else:
        (x_ref, sh_ref, sc_ref, g_ref, wm_ref, wvt_ref, mavg_ref, gn_ref,
         k_ref, vt_ref, u_ref) = refs
    h = _rms_modulate(x_ref[...], g_ref[...], sc_ref[...], sh_ref[...])
    hb = h.astype(BF16)
    p = jnp.dot(hb, wm_ref[...], preferred_element_type=F32)
    qk = p[:, :n_norm]
    u_ref[...] = p[:, n_norm:]
    sq = (qk * qk).astype(BF16)
    mavg = mavg_ref[...]
    width = mavg.shape[0]
    msq = jnp.concatenate(
        [jnp.dot(sq[:, width * j:width * (j + 1)], mavg, preferred_element_type=F32)
         for j in range(n_norm // width)], axis=1)
    qkn = qk * lax.rsqrt(msq + NORM_EPS) * gn_ref[...]
    n_q_chunks = n_norm // LANES - k_ref.shape[0]
    if rope:
        lane = lax.broadcasted_iota(jnp.int32, (qkn.shape[0], LANES), 1)
        half = ROPE_AXIS_DIM // 2
        first_half = (lane % ROPE_AXIS_DIM) < half
        cos = cos_ref[...]
        sin = sin_ref[...]
    for c in range(n_norm // LANES):
        xc = qkn[:, LANES * c:LANES * (c + 1)]
        if rope:
            partner = jnp.where(first_half, pltpu.roll(xc, LANES - half, 1), pltpu.roll(xc, half, 1))
            xc = xc * cos + partner * sin
        if c < n_q_chunks:
            q_ref[c] = xc.astype(BF16)
        else:
            k_ref[c - n_q_chunks] = xc.astype(BF16)
    vt = lax.dot_general(wvt_ref[...], hb, (((1,), (1,)), ((), ())), preferred_element_type=F32)
    for g in range(N_KV_HEADS):
        vt_ref[g] = vt[HEAD_DIM * g:HEAD_DIM * (g + 1), :].astype(BF16)


def _project(x, sh, sc, g, wm, wvt, mavg, gn, cos, sin):
    bsz, n, _ = x.shape
    tm = PROJ_ROWS
    rope = cos is not None
    n_norm = gn.shape[1]
    ncol = wm.shape[1]
    kv_w = N_KV_HEADS * HEAD_DIM
    q_chunks = D_MODEL // LANES
    mod_rows = sh.shape[0]
    mod_spec = pl.BlockSpec((None, 1, D_MODEL),
                            (lambda i, b: (b, 0, 0)) if mod_rows == bsz else (lambda i, b: (0, 0, 0)))
    in_specs = [pl.BlockSpec((None, tm, D_MODEL), lambda i, b: (b, i, 0)),
                mod_spec, mod_spec,
                _const_spec((1, D_MODEL)),
                _const_spec((D_MODEL, ncol)),
                _const_spec((kv_w, D_MODEL)),
                _const_spec(mavg.shape),
                _const_spec((1, n_norm))]
    args = [x, sh, sc, g, wm, wvt, mavg, gn]
    out_specs = [pl.BlockSpec((None, N_KV_HEADS, tm, LANES), lambda i, b: (b, 0, i, 0)),
                 pl.BlockSpec((None, N_KV_HEADS, HEAD_DIM, tm), lambda i, b: (b, 0, 0, i)),
                 pl.BlockSpec((None, tm, D_SSM), lambda i, b: (b, i, 0))]
    out_shape = [jax.ShapeDtypeStruct((bsz, N_KV_HEADS, n, LANES), BF16),
                 jax.ShapeDtypeStruct((bsz, N_KV_HEADS, HEAD_DIM, n), BF16),
                 jax.ShapeDtypeStruct((bsz, n, D_SSM), F32)]
    if rope:
        in_specs += [pl.BlockSpec((tm, LANES), lambda i, b: (i, 0))] * 2
        args += [cos, sin]
        out_specs = [pl.BlockSpec((None, q_chunks, tm, LANES), lambda i, b: (b, 0, i, 0))] + out_specs
        out_shape = [jax.ShapeDtypeStruct((bsz, q_chunks, n, LANES), BF16)] + out_shape
    outs = pl.pallas_call(
        functools.partial(_proj_body, n_norm=n_norm, rope=rope),
        grid=(n // tm, bsz),
        in_specs=in_specs,
        out_specs=out_specs,
        out_shape=out_shape,
        compiler_params=_params(2),
        name="proj_latent" if rope else "proj_ctx",
    )(*args)
    return outs if rope else [None] + list(outs)


def _attn_body(q_ref, kl_ref, kc_ref, vtl_ref, vtc_ref, o_ref, s0_scr, s1_scr, m_scr, o_scr):
    n_ctx = kc_ref.shape[1]
    n_lat = kl_ref.shape[1]
    tq = q_ref.shape[1]
    lane = lax.broadcasted_iota(jnp.int32, (tq, LANES), 1)
    head_lanes = (lane < HEAD_DIM, lane >= HEAD_DIM)
    s_bufs = (s0_scr, s1_scr)
    nt = (((1,), (1,)), ((), ()))
    reload_base = jnp.minimum(pl.program_id(1), 0)

    def slab_max(s):
        return jnp.max(s.reshape(s.shape[0] // SUBLANES, SUBLANES, tq), axis=0)

    def scores(g, r, slot):
        s_scr = s_bufs[slot]
        qc = q_ref[2 * g + r // 2]
        qm = jnp.where(head_lanes[r % 2], qc, jnp.zeros_like(qc))
        sc = lax.dot_general(kc_ref[g], qm, nt, preferred_element_type=F32)
        sl = lax.dot_general(kl_ref[g], qm, nt, preferred_element_type=F32)
        s_scr[:n_ctx, :] = sc
        s_scr[n_ctx:, :] = sl
        m_scr[slot] = jnp.maximum(slab_max(sc), slab_max(sl))

    def finish(g, r, slot):
        s_scr = s_bufs[slot]
        m = jnp.max(m_scr[slot], axis=0, keepdims=True)
        m8 = jnp.broadcast_to(m, (SUBLANES, tq))
        l8 = jnp.zeros((SUBLANES, tq), F32)
        acc = jnp.zeros((HEAD_DIM, tq), F32)
        for kb in range((n_ctx + n_lat) // ATTN_KEYS):
            lo = kb * ATTN_KEYS
            row = pl.multiple_of(reload_base + lo, ATTN_KEYS)
            s = s_scr[pl.ds(row, ATTN_KEYS), :].reshape(ATTN_KEYS // SUBLANES, SUBLANES, tq)
            p = jnp.exp2(s - m8)
            l8 = l8 + jnp.sum(p, axis=0)
            pb = p.reshape(ATTN_KEYS, tq).astype(BF16)
            if lo < n_ctx:
                vt = vtc_ref[g, :, lo:lo + ATTN_KEYS]
            else:
                vt = vtl_ref[g, :, lo - n_ctx:lo - n_ctx + ATTN_KEYS]
            acc = acc + jnp.dot(vt, pb, preferred_element_type=F32)
        o_scr[KV_REP * g + r] = acc / jnp.sum(l8, axis=0, keepdims=True)

    scores(0, 0, 0)

    def group(g, carry):
        for r in range(KV_REP):
            if r + 1 < KV_REP:
                scores(g, r + 1, (r + 1) % 2)
            else:
                scores(jnp.minimum(g + 1, N_KV_HEADS - 1), 0, (r + 1) % 2)
            finish(g, r, r % 2)
        return carry

    lax.fori_loop(0, N_KV_HEADS, group, 0)
    o_ref[...] = o_scr[...].reshape(N_HEADS * HEAD_DIM, tq).T.astype(BF16)


def _attention(q, k_l, k_c, vt_l, vt_c):
    bsz, q_chunks, n, _ = q.shape
    n_ctx = k_c.shape[2]
    tq = ATTN_ROWS
    whole = lambda shape: pl.BlockSpec((None,) + shape, lambda b, j: (b,) + (0,) * len(shape))
    return pl.pallas_call(
        _attn_body,
        grid=(bsz, n // tq),
        in_specs=[pl.BlockSpec((None, q_chunks, tq, LANES), lambda b, j: (b, 0, j, 0)),
                  whole((N_KV_HEADS, n, LANES)),
                  whole((N_KV_HEADS, n_ctx, LANES)),
                  whole((N_KV_HEADS, HEAD_DIM, n)),
                  whole((N_KV_HEADS, HEAD_DIM, n_ctx))],
        out_specs=pl.BlockSpec((None, tq, D_MODEL), lambda b, j: (b, j, 0)),
        out_shape=jax.ShapeDtypeStruct((bsz, n, D_MODEL), BF16),
        scratch_shapes=[pltpu.VMEM((n_ctx + n, tq), F32),
                        pltpu.VMEM((n_ctx + n, tq), F32),
                        pltpu.VMEM((2, SUBLANES, tq), F32),
                        pltpu.VMEM((N_HEADS, HEAD_DIM, tq), F32)],
        compiler_params=_params(2),
        name="attn",
    )(q, k_l, k_c, vt_l, vt_c)


def _ssm_body(uc_ref, ul_ref, bm_ref, cm_ref, ar_ref, ai_ref, y_ref,
              ut_scr, yt_scr, bu_scr, h_scr, hr_scr, hi_scr, *, n_ctx_tiles):
    d = pl.program_id(0)
    i = pl.program_id(2)
    steps = uc_ref.shape[1]
    half_w = N_STATE // 2
    n_slabs = D_SSM // LANES

    @pl.when(i == 0)
    def _():
        hr_scr[...] = jnp.zeros_like(hr_scr)
        hi_scr[...] = jnp.zeros_like(hi_scr)

    u = jnp.where(i < n_ctx_tiles, uc_ref[...], ul_ref[...])
    for b in range(SSM_BATCH):
        for c in range(n_slabs):
            ut_scr[c, pl.ds(b, steps, stride=SSM_BATCH), :] = u[b, :, LANES * c:LANES * (c + 1)]
    ub = jnp.concatenate([ut_scr[c] for c in range(n_slabs)], axis=1).astype(BF16)
    for k in range(2):
        r = jnp.dot(ub[:, 256 * k:256 * (k + 1)], bm_ref[k], preferred_element_type=F32)
        bu_scr[:, half_w * k:half_w * (k + 1)] = r[:, :half_w]
        bu_scr[:, N_STATE + half_w * k:N_STATE + half_w * (k + 1)] = r[:, half_w:]

    def scan(forward):
        pairs = steps // 2
        for c in range(N_STATE // SCAN_LANES):
            re = slice(SCAN_LANES * c, SCAN_LANES * (c + 1))
            im = slice(N_STATE + SCAN_LANES * c, N_STATE + SCAN_LANES * (c + 1))
            a_r = ar_ref[:, re]
            a_i = ai_ref[:, re]

            def advance(hr, hi, row):
                br = bu_scr[pl.ds(row, SSM_BATCH), re]
                bi = bu_scr[pl.ds(row, SSM_BATCH), im]
                return a_r * hr - a_i * hi + br, a_r * hi + a_i * hr + bi

            def step(j, carry):
                hr, hi = carry
                jj = j if forward else pairs - 1 - j
                r0 = pl.multiple_of(jj * 2 * SSM_BATCH, 2 * SSM_BATCH)
                r1 = pl.multiple_of(r0 + SSM_BATCH, SSM_BATCH)
                first, second = (r0, r1) if forward else (r1, r0)
                hr1, hi1 = advance(hr, hi, first)
                hr2, hi2 = advance(hr1, hi1, second)
                top_r, bot_r = (hr1, hr2) if forward else (hr2, hr1)
                top_i, bot_i = (hi1, hi2) if forward else (hi2, hi1)
                h_scr[pl.ds(r0, 2 * SSM_BATCH), re] = jnp.concatenate([top_r, bot_r], axis=0).astype(BF16)
                h_scr[pl.ds(r0, 2 * SSM_BATCH), im] = jnp.concatenate([top_i, bot_i], axis=0).astype(BF16)
                return hr2, hi2

            hr, hi = lax.fori_loop(0, pairs, step, (hr_scr[:, re], hi_scr[:, re]))
            hr_scr[:, re] = hr
            hi_scr[:, re] = hi

    @pl.when(d == 0)
    def _():
        scan(True)

    @pl.when(d == 1)
    def _():
        scan(False)

    @pl.when(i >= n_ctx_tiles)
    def _():
        for k in range(2):
            y = (jnp.dot(h_scr[:, half_w * k:half_w * (k + 1)], cm_ref[k, 0], preferred_element_type=F32)
                 + jnp.dot(h_scr[:, N_STATE + half_w * k:N_STATE + half_w * (k + 1)], cm_ref[k, 1],
                           preferred_element_type=F32))
            for c in range(2):
                yt_scr[2 * k + c] = y[:, LANES * c:LANES * (c + 1)]
        for b in range(SSM_BATCH):
            y_ref[b] = jnp.concatenate(
                [yt_scr[c, pl.ds(b, steps, stride=SSM_BATCH), :] for c in range(n_slabs)], axis=1)


def _ssm(u_c, u_l, bm, cm, a_re, a_im):
    bsz, n_ctx, _ = u_c.shape
    n = u_l.shape[1]
    t = SSM_STEPS
    nct = n_ctx // t
    nlt = n // t
    rows = t * SSM_BATCH

    def ctx_tile(d, i):
        return jnp.where(d == 0, jnp.minimum(i, nct - 1), jnp.maximum(nct - 1 - i, 0))

    def lat_tile(d, i):
        j = jnp.maximum(i - nct, 0)
        return jnp.where(d == 0, j, nlt - 1 - j)

    return pl.pallas_call(
        functools.partial(_ssm_body, n_ctx_tiles=nct),
        grid=(2, bsz // SSM_BATCH, nct + nlt),
        in_specs=[pl.BlockSpec((SSM_BATCH, t, D_SSM), lambda d, bh, i: (bh, ctx_tile(d, i), 0)),
                  pl.BlockSpec((SSM_BATCH, t, D_SSM), lambda d, bh, i: (bh, lat_tile(d, i), 0)),
                  pl.BlockSpec((None, 2, 256, N_STATE), lambda d, bh, i: (d, 0, 0, 0)),
                  pl.BlockSpec((None, 2, 2, N_STATE // 2, 256), lambda d, bh, i: (d, 0, 0, 0, 0)),
                  pl.BlockSpec((None, SUBLANES, N_STATE), lambda d, bh, i: (d, 0, 0)),
                  pl.BlockSpec((None, SUBLANES, N_STATE), lambda d, bh, i: (d, 0, 0))],
        out_specs=pl.BlockSpec((None, SSM_BATCH, t, D_SSM), lambda d, bh, i: (d, bh, lat_tile(d, i), 0)),
        out_shape=jax.ShapeDtypeStruct((2, bsz, n, D_SSM), F32),
        scratch_shapes=[pltpu.VMEM((D_SSM // LANES, rows, LANES), F32),
                        pltpu.VMEM((D_SSM // LANES, rows, LANES), F32),
                        pltpu.VMEM((rows, 2 * N_STATE), F32),
                        pltpu.VMEM((rows, 2 * N_STATE), BF16),
                        pltpu.VMEM((SSM_BATCH, N_STATE), F32),
                        pltpu.VMEM((SSM_BATCH, N_STATE), F32)],
        compiler_params=_params(3),
        name="ssm",
    )(u_c, u_l, bm, cm, a_re, a_im)


def _merge_body(x_ref, yf_ref, yb_ref, u_ref, at_ref,
                sh1_ref, sc1_ref, g1_ref, sh2_ref, sc2_ref, g2_ref, n1_ref, n2_ref,
                wg_ref, dv_ref, wglu_ref, bglu_ref, wba_ref, wbs_ref, wo_ref,
                w1_ref, b1_ref, w2_ref, b2_ref, o_ref):
    x = x_ref[...]
    hb = _rms_modulate(x, n1_ref[...], sc1_ref[...], sh1_ref[...]).astype(BF16)
    gates = jnp.dot(hb, wg_ref[...], preferred_element_type=F32)
    gate_a = jax.nn.sigmoid(gates[:, :D_MODEL])
    gate_s = jax.nn.sigmoid(gates[:, D_MODEL:])

    y = yf_ref[...] + yb_ref[...] + dv_ref[...] * u_ref[...]
    y = jax.nn.gelu(y)
    z = jnp.dot(y.astype(BF16), wglu_ref[...], preferred_element_type=F32) + bglu_ref[...]
    y = y * jax.nn.sigmoid(z)

    merged = (gate_a * jnp.dot(at_ref[...], wba_ref[...], preferred_element_type=F32)
              + gate_s * jnp.dot(y.astype(BF16), wbs_ref[...], preferred_element_type=F32))
    mix = jnp.dot(merged.astype(BF16), wo_ref[...], preferred_element_type=F32)
    x1 = x + g1_ref[...] * mix

    h2 = _rms_modulate(x1, n2_ref[...], sc2_ref[...], sh2_ref[...]).astype(BF16)
    acc = jnp.zeros_like(x1)
    for c in range(D_FF // FF_CHUNK):
        cols = slice(FF_CHUNK * c, FF_CHUNK * (c + 1))
        hid = jnp.dot(h2, w1_ref[:, cols], preferred_element_type=F32) + b1_ref[:, cols]
        hid = jnp.square(jnp.maximum(hid, 0.0))
        acc = acc + jnp.dot(hid.astype(BF16), w2_ref[cols, :], preferred_element_type=F32)
    o_ref[...] = x1 + g2_ref[...] * (acc + b2_ref[...])


def _merge(x, y2, u2, attn, mods, n1, n2, wg, dvec, wglu, bglu, wba, wbs, wo, w1, b1, w2, b2):
    bsz, n, _ = x.shape
    tm = MERGE_ROWS
    row_spec = pl.BlockSpec((None, tm, D_MODEL), lambda b, i: (b, i, 0))
    mod_spec = pl.BlockSpec((None, 1, D_MODEL), lambda b, i: (b, 0, 0))
    in_specs = [row_spec,
                pl.BlockSpec((None, None, tm, D_SSM), lambda b, i: (0, b, i, 0)),
                pl.BlockSpec((None, None, tm, D_SSM), lambda b, i: (1, b, i, 0)),
                pl.BlockSpec((None, tm, D_SSM), lambda b, i: (b, i, 0)),
                row_spec] + [mod_spec] * 6
    consts = [n1, n2, wg, dvec, wglu, bglu, wba, wbs, wo, w1, b1, w2, b2]
    in_specs += [_const_spec(a.shape) for a in consts]
    return pl.pallas_call(
        _merge_body,
        grid=(bsz, n // tm),
        in_specs=in_specs,
        out_specs=row_spec,
        out_shape=jax.ShapeDtypeStruct(x.shape, x.dtype),
        compiler_params=_params(2),
        name="merge",
    )(x, y2, y2, u2, attn, *mods, *consts)


def _rope_tables(n_tok):
    half = ROPE_AXIS_DIM // 2
    pos = jnp.arange(n_tok, dtype=jnp.int32)
    row = (pos // GRID_W).astype(F32)
    col = (pos % GRID_W).astype(F32)
    inv_freq = ROPE_THETA ** (-jnp.arange(half, dtype=F32) / half)
    ang_r = row[:, None] * inv_freq
    ang_c = col[:, None] * inv_freq
    cos = jnp.concatenate([jnp.cos(ang_r)] * 2 + [jnp.cos(ang_c)] * 2, axis=-1)
    sin = jnp.concatenate([-jnp.sin(ang_r), jnp.sin(ang_r), -jnp.sin(ang_c), jnp.sin(ang_c)], axis=-1)
    reps = LANES // HEAD_DIM
    return jnp.tile(cos, (1, reps)), jnp.tile(sin, (1, reps))


def kernel(x, c, ctx, c_ctx, w_mod, b_mod, norm1_g, norm2_g, w_in, q_norm_g, k_norm_g,
           ssm_lambda_re, ssm_lambda_im, ssm_log_dt, ssm_b_re, ssm_b_im, ssm_c_re, ssm_c_im,
           ssm_d, w_glu, b_glu, w_br_attn, w_br_ssm, w_out, w_mlp1, b_mlp1, w_mlp2, b_mlp2):
    bsz, n_tok, d_model = x.shape
    n_ctx = ctx.shape[1]
    assert d_model == D_MODEL and w_mod.shape[0] == 1, "single-layer block of width 1024 only"
    assert n_tok % PROJ_ROWS == 0 and n_ctx % PROJ_ROWS == 0 and n_ctx % SSM_STEPS == 0
    assert bsz % SSM_BATCH == 0 and n_tok % GRID_W == 0

    pad = (-(bsz + 1)) % SUBLANES
    c_rows = jnp.concatenate([c, c_ctx[None, :], jnp.zeros((pad, D_MODEL), F32)], axis=0)
    mod = _modulation(c_rows, w_mod[0], b_mod[0][None, :])
    sh1, sc1, g1, sh2, sc2, g2 = [m[:, None, :] for m in jnp.split(mod, 6, axis=-1)]
    lat = lambda m: m[:bsz]
    cx = lambda m: m[bsz:bsz + 1]

    w = w_in[0]
    w_k = w[:, OFF_K:OFF_V].reshape(D_MODEL, N_KV_HEADS, 1, HEAD_DIM)
    w_kdup = jnp.broadcast_to(w_k, (D_MODEL, N_KV_HEADS, 2, HEAD_DIM)).reshape(D_MODEL, 2 * OFF_V - 2 * OFF_K)
    w_u = w[:, OFF_U:OFF_GA]
    wm_lat = jnp.concatenate([w[:, :OFF_K], w_kdup, w_u], axis=1).astype(BF16)
    wm_ctx = jnp.concatenate([w_kdup, w_u], axis=1).astype(BF16)
    w_vt = w[:, OFF_V:OFF_U].T.astype(BF16)
    w_gate = w[:, OFF_GA:].astype(BF16)
    mavg = jnp.kron(jnp.eye(MXU_TILE // HEAD_DIM, dtype=F32),
                    jnp.full((HEAD_DIM, HEAD_DIM), 1.0 / HEAD_DIM, F32)).astype(BF16)
    gk = jnp.tile(k_norm_g[0], 2 * N_KV_HEADS)
    gq = jnp.tile(q_norm_g[0], N_HEADS) * (HEAD_DIM ** -0.5 * LOG2_E)
    gn_lat = jnp.concatenate([gq, gk])[None, :]
    gn_ctx = gk[None, :]
    cos, sin = _rope_tables(n_tok)
    n1 = norm1_g[0][None, :]

    q, k_l, vt_l, u_l = _project(x, lat(sh1), lat(sc1), n1, wm_lat, w_vt, mavg, gn_lat, cos, sin)
    _, k_c, vt_c, u_c = _project(ctx, cx(sh1), cx(sc1), n1, wm_ctx, w_vt, mavg, gn_ctx, None, None)

    attn = _attention(q, k_l, k_c, vt_l, vt_c)

    a_re, a_im, bb_re, bb_im = _discretize(ssm_lambda_re[0], ssm_lambda_im[0], ssm_log_dt[0],
                                           ssm_b_re[0], ssm_b_im[0])
    bm, cm, a_re8, a_im8 = _ssm_matrices(a_re, a_im, bb_re, bb_im, ssm_c_re[0], ssm_c_im[0])
    y = _ssm(u_c, u_l, bm, cm, a_re8, a_im8)

    mods = [lat(m) for m in (sh1, sc1, g1, sh2, sc2, g2)]
    return _merge(x, y, u_l, attn, mods, n1, norm2_g[0][None, :],
                  w_gate, ssm_d[0][None, :], w_glu[0].astype(BF16), b_glu[0][None, :],
                  w_br_attn[0].astype(BF16), w_br_ssm[0].astype(BF16), w_out[0].astype(BF16),
                  w_mlp1[0].astype(BF16), b_mlp1[0][None, :], w_mlp2[0].astype(BF16), b_mlp2[0][None, :])
```

```python
import functools

import jax
import jax.numpy as jnp
from jax import lax
from jax.experimental import pallas as pl
from jax.experimental.pallas import tpu as pltpu

F32 = jnp.float32
BF16 = jnp.bfloat16

D_MODEL = 1024
N_HEADS = 16
N_KV_HEADS = 4
HEAD_DIM = 64
KV_REP = N_HEADS // N_KV_HEADS
GRID_W = 64
ROPE_THETA = 10000.0
ROPE_AXIS_DIM = HEAD_DIM // 2
D_SSM = 512
SSM_GROUP = 16
N_SSM_GROUPS = 32
SSM_STATE = 64
N_STATE = N_SSM_GROUPS * SSM_STATE
D_FF = 4 * D_MODEL
NORM_EPS = 1e-6
OFF_K = 1024
OFF_V = 1280
OFF_U = 1536
OFF_GA = 2048

LANES = 128
SUBLANES = 8
MXU_TILE = 256
LOG2_E = 1.4426950408889634
VMEM_LIMIT_BYTES = 56 * 1024 * 1024

PROJ_ROWS = 256
ATTN_ROWS = 256
ATTN_KEYS = 256
ONES_ROWS = 16
MERGE_ROWS = 256
SSM_STEPS = 64
SSM_BATCH = SUBLANES
SCAN_LANES = 1024
FF_CHUNK = 1024


def _const_spec(shape):
    nd = len(shape)
    return pl.BlockSpec(shape, lambda *_: (0,) * nd, pipeline_mode=pl.Buffered(1))


def _params(n_axes, vmem=VMEM_LIMIT_BYTES, flags=None):
    return pltpu.CompilerParams(dimension_semantics=("arbitrary",) * n_axes, vmem_limit_bytes=vmem,
                                flags=flags)


def _mod_body(c_ref, w_ref, b_ref, o_ref):
    c = c_ref[...]
    s = c * jax.nn.sigmoid(c)
    o_ref[...] = jnp.dot(s, w_ref[...], preferred_element_type=F32) + b_ref[...]


def _modulation(c_rows, w_mod, b_mod):
    rows = c_rows.shape[0]
    ncol = w_mod.shape[1]
    blk = D_MODEL
    return pl.pallas_call(
        _mod_body,
        grid=(ncol // blk,),
        in_specs=[pl.BlockSpec((rows, D_MODEL), lambda j: (0, 0)),
                  pl.BlockSpec((D_MODEL, blk), lambda j: (0, j)),
                  pl.BlockSpec((1, blk), lambda j: (0, j))],
        out_specs=pl.BlockSpec((rows, blk), lambda j: (0, j)),
        out_shape=jax.ShapeDtypeStruct((rows, ncol), F32),
        compiler_params=_params(1),
        name="mod",
    )(c_rows, w_mod, b_mod)


def _disc_body(lr_ref, li_ref, ldt_ref, br_ref, bi_ref, ar_ref, ai_ref, bbr_ref, bbi_ref):
    lr = lr_ref[...]
    li = li_ref[...]
    dt = jnp.exp(ldt_ref[...])
    mag = jnp.exp(lr * dt)
    ar = mag * jnp.cos(li * dt)
    ai = mag * jnp.sin(li * dt)
    nr = ar - 1.0
    den = lr * lr + li * li
    cr = (nr * lr + ai * li) / den
    ci = (ai * lr - nr * li) / den
    br = br_ref[...]
    bi = bi_ref[...]
    ar_ref[...] = ar
    ai_ref[...] = ai
    bbr_ref[...] = cr * br - ci * bi
    bbi_ref[...] = cr * bi + ci * br


def _discretize(lam_re, lam_im, log_dt, b_re, b_im):
    rows = 2 * N_SSM_GROUPS
    cols = SSM_STATE * SSM_GROUP
    rep = lambda a: jnp.repeat(a.reshape(rows, SSM_STATE), SSM_GROUP, axis=1)
    ldt = jnp.broadcast_to(log_dt.reshape(rows, 1), (rows, cols))
    spec = pl.BlockSpec((rows, cols), lambda: (0, 0))
    sds = jax.ShapeDtypeStruct((rows, cols), F32)
    ar, ai, bbr, bbi = pl.pallas_call(
        _disc_body,
        in_specs=[spec] * 5,
        out_specs=[spec] * 4,
        out_shape=[sds] * 4,
        name="disc",
    )(rep(lam_re), rep(lam_im), ldt, b_re.reshape(rows, cols), b_im.reshape(rows, cols))
    a_re = ar[:, ::SSM_GROUP].reshape(2, N_STATE)
    a_im = ai[:, ::SSM_GROUP].reshape(2, N_STATE)
    shape = (2, N_SSM_GROUPS, SSM_STATE, SSM_GROUP)
    return a_re, a_im, bbr.reshape(shape), bbi.reshape(shape)


def _ssm_matrices(a_re, a_im, bb_re, bb_im, c_re, c_im):
    eye = jnp.eye(16, dtype=F32)

    def in_blocks(bb):
        t = bb.reshape(2, 2, 16, SSM_STATE, SSM_GROUP).transpose(0, 1, 2, 4, 3)
        return jnp.einsum('dkghp,gG->dkghGp', t, eye).reshape(2, 2, 256, 1024)

    def out_blocks(c):
        t = c.reshape(2, 2, 16, SSM_GROUP, SSM_STATE).transpose(0, 1, 2, 4, 3)
        return jnp.einsum('dkgph,gG->dkgpGh', t, eye).reshape(2, 2, 1024, 256)

    bm = jnp.concatenate([in_blocks(bb_re), in_blocks(bb_im)], axis=-1).astype(BF16)
    cm = jnp.stack([out_blocks(c_re), out_blocks(-c_im)], axis=2).astype(BF16)
    bcast = lambda a: jnp.broadcast_to(a[:, None, :], (2, SUBLANES, N_STATE))
    return bm, cm, bcast(a_re), bcast(a_im)


def _rms_modulate(x, g, sc, sh):
    ms = jnp.mean(x * x, axis=-1, keepdims=True)
    xn = x * lax.rsqrt(ms + NORM_EPS) * g
    return xn * (1.0 + sc) + sh


def _proj_body(*refs, n_norm, rope):
    if rope:
        (x_ref, sh_ref, sc_ref, g_ref, wm_ref, wvt_ref, mavg_ref, gn_ref, cos_ref, sin_ref,
         q_ref, k_ref, vt_ref, u_ref) = refs
    else:
        (x_ref, sh_ref, sc_ref, g_ref, wm_ref, wvt_ref, mavg_ref, gn_ref,
         k_ref, vt_ref, u_ref) = refs
    h = _rms_modulate(x_ref[...], g_ref[...], sc_ref[...], sh_ref[...])
    hb = h.astype(BF16)
    p = jnp.dot(hb, wm_ref[...], preferred_element_type=F32)
    qk = p[:, :n_norm]
    u_ref[...] = p[:, n_norm:]
    sq = (qk * qk).astype(BF16)
    mavg = mavg_ref[...]
    width = mavg.shape[0]
    msq = jnp.concatenate(
        [jnp.dot(sq[:, width * j:width * (j + 1)], mavg, preferred_element_type=F32)
         for j in range(n_norm // width)], axis=1)
    qkn = qk * lax.rsqrt(msq + NORM_EPS) * gn_ref[...]
    n_q_chunks = n_norm // LANES - k_ref.shape[0]
    if rope:
        lane = lax.broadcasted_iota(jnp.int32, (qkn.shape[0], LANES), 1)
        half = ROPE_AXIS_DIM // 2
        first_half = (lane % ROPE_AXIS_DIM) < half
        cos = cos_ref[...]
        sin = sin_ref[...]
    for c in range(n_norm // LANES):
        xc = qkn[:, LANES * c:LANES * (c + 1)]
        if rope:
            partner = jnp.where(first_half, pltpu.roll(xc, LANES - half, 1), pltpu.roll(xc, half, 1))
            xc = xc * cos + partner * sin
        if c < n_q_chunks:
            q_ref[c] = xc.astype(BF16)
        else:
            k_ref[c - n_q_chunks] = xc.astype(BF16)
    vt = lax.dot_general(wvt_ref[...], hb, (((1,), (1,)), ((), ())), preferred_element_type=F32)
    for g in range(N_KV_HEADS):
        vt_ref[g] = vt[HEAD_DIM * g:HEAD_DIM * (g + 1), :].astype(BF16)


def _project(x, sh, sc, g, wm, wvt, mavg, gn, cos, sin):
    bsz, n, _ = x.shape
    tm = PROJ_ROWS
    rope = cos is not None
    n_norm = gn.shape[1]
    ncol = wm.shape[1]
    kv_w = N_KV_HEADS * HEAD_DIM
    q_chunks = D_MODEL // LANES
    mod_rows = sh.shape[0]
    mod_spec = pl.BlockSpec((None, 1, D_MODEL),
                            (lambda i, b: (b, 0, 0)) if mod_rows == bsz else (lambda i, b: (0, 0, 0)))
    in_specs = [pl.BlockSpec((None, tm, D_MODEL), lambda i, b: (b, i, 0)),
                mod_spec, mod_spec,
                _const_spec((1, D_MODEL)),
                _const_spec((D_MODEL, ncol)),
                _const_spec((kv_w, D_MODEL)),
                _const_spec(mavg.shape),
                _const_spec((1, n_norm))]
    args = [x, sh, sc, g, wm, wvt, mavg, gn]
    out_specs = [pl.BlockSpec((None, N_KV_HEADS, tm, LANES), lambda i, b: (b, 0, i, 0)),
                 pl.BlockSpec((None, N_KV_HEADS, HEAD_DIM, tm), lambda i, b: (b, 0, 0, i)),
                 pl.BlockSpec((None, tm, D_SSM), lambda i, b: (b, i, 0))]
    out_shape = [jax.ShapeDtypeStruct((bsz, N_KV_HEADS, n, LANES), BF16),
                 jax.ShapeDtypeStruct((bsz, N_KV_HEADS, HEAD_DIM, n), BF16),
                 jax.ShapeDtypeStruct((bsz, n, D_SSM), F32)]
    if rope:
        in_specs += [pl.BlockSpec((tm, LANES), lambda i, b: (i, 0))] * 2
        args += [cos, sin]
        out_specs = [pl.BlockSpec((None, q_chunks, tm, LANES), lambda i, b: (b, 0, i, 0))] + out_specs
        out_shape = [jax.ShapeDtypeStruct((bsz, q_chunks, n, LANES), BF16)] + out_shape
    outs = pl.pallas_call(
        functools.partial(_proj_body, n_norm=n_norm, rope=rope),
        grid=(n // tm, bsz),
        in_specs=in_specs,
        out_specs=out_specs,
        out_shape=out_shape,
        compiler_params=_params(2),
        name="proj_latent" if rope else "proj_ctx",
    )(*args)
    return outs if rope else [None] + list(outs)


def _attn_body(q_ref, kl_ref, kc_ref, vtl_ref, vtc_ref, o_ref, s0_scr, s1_scr, m_scr, o_scr):
    n_ctx = kc_ref.shape[1]
    n_lat = kl_ref.shape[1]
    n_blocks = (n_ctx + n_lat) // ATTN_KEYS
    tq = q_ref.shape[1]
    lane = lax.broadcasted_iota(jnp.int32, (tq, LANES), 1)
    head_lanes = (lane < HEAD_DIM, lane >= HEAD_DIM)
    s_bufs = (s0_scr, s1_scr)
    nt = (((1,), (1,)), ((), ()))
    reload_base = jnp.minimum(pl.program_id(1), 0)

    def slab_max(s):
        return jnp.max(s.reshape(s.shape[0] // SUBLANES, SUBLANES, tq), axis=0)

    def scores(g, r, slot):
        s_scr = s_bufs[slot]
        qc = q_ref[2 * g + r // 2]
        qm = jnp.where(head_lanes[r % 2], qc, jnp.zeros_like(qc))
        sc = lax.dot_general(kc_ref[g], qm, nt, preferred_element_type=F32)
        sl = lax.dot_general(kl_ref[g], qm, nt, preferred_element_type=F32)
        s_scr[:n_ctx, :] = sc
        s_scr[n_ctx:, :] = sl
        m_scr[slot] = jnp.maximum(slab_max(sc), slab_max(sl))

    ones_rows = jnp.ones((ONES_ROWS, ATTN_KEYS), BF16)

    def finish(g, r, slot):
        s_scr = s_bufs[slot]
        m = jnp.max(m_scr[slot], axis=0, keepdims=True)
        m8 = jnp.broadcast_to(m, (SUBLANES, tq))
        acc = jnp.zeros((HEAD_DIM + ONES_ROWS, tq), F32)
        for kb in range(n_blocks):
            lo = kb * ATTN_KEYS
            row = pl.multiple_of(reload_base + lo, ATTN_KEYS)
            s = s_scr[pl.ds(row, ATTN_KEYS), :].reshape(ATTN_KEYS // SUBLANES, SUBLANES, tq)
            pb = jnp.exp2(s - m8).reshape(ATTN_KEYS, tq).astype(BF16)
            if lo < n_ctx:
                vt = vtc_ref[g, :, lo:lo + ATTN_KEYS]
            else:
                vt = vtl_ref[g, :, lo - n_ctx:lo - n_ctx + ATTN_KEYS]
            acc = acc + jnp.dot(jnp.concatenate([vt, ones_rows], axis=0), pb, preferred_element_type=F32)
        o_scr[KV_REP * g + r] = acc

    scores(0, 0, 0)

    def group(g, carry):
        for r in range(KV_REP):
            if r + 1 < KV_REP:
                scores(g, r + 1, (r + 1) % 2)
            else:
                scores(jnp.minimum(g + 1, N_KV_HEADS - 1), 0, (r + 1) % 2)
            finish(g, r, r % 2)
        return carry

    lax.fori_loop(0, N_KV_HEADS, group, 0)
    o_all = o_scr[...]
    o_all = o_all[:, :HEAD_DIM, :] / o_all[:, HEAD_DIM:HEAD_DIM + 1, :]
    o_ref[...] = o_all.reshape(N_HEADS * HEAD_DIM, tq).T.astype(BF16)


def _attention(q, k_l, k_c, vt_l, vt_c):
    bsz, q_chunks, n, _ = q.shape
    n_ctx = k_c.shape[2]
    tq = ATTN_ROWS
    whole = lambda shape: pl.BlockSpec((None,) + shape, lambda b, j: (b,) + (0,) * len(shape))
    return pl.pallas_call(
        _attn_body,
        grid=(bsz, n // tq),
        in_specs=[pl.BlockSpec((None, q_chunks, tq, LANES), lambda b, j: (b, 0, j, 0)),
                  whole((N_KV_HEADS, n, LANES)),
                  whole((N_KV_HEADS, n_ctx, LANES)),
                  whole((N_KV_HEADS, HEAD_DIM, n)),
                  whole((N_KV_HEADS, HEAD_DIM, n_ctx))],
        out_specs=pl.BlockSpec((None, tq, D_MODEL), lambda b, j: (b, j, 0)),
        out_shape=jax.ShapeDtypeStruct((bsz, n, D_MODEL), BF16),
        scratch_shapes=[pltpu.VMEM((n_ctx + n, tq), F32),
                        pltpu.VMEM((n_ctx + n, tq), F32),
                        pltpu.VMEM((2, SUBLANES, tq), F32),
                        pltpu.VMEM((N_HEADS, HEAD_DIM + ONES_ROWS, tq), F32)],
        compiler_params=_params(2),
        name="attn",
    )(q, k_l, k_c, vt_l, vt_c)


def _ssm_body(uc_ref, ul_ref, bm_ref, cm_ref, ar_ref, ai_ref, y_ref,
              ut_scr, yt_scr, bu_a, bu_b, h_a, h_b, hr_scr, hi_scr, *, n_ctx_pairs):
    d = pl.program_id(0)
    s = pl.program_id(2)
    steps = uc_ref.shape[1] // 2
    rows = steps * SSM_BATCH
    half_w = N_STATE // 2
    n_slabs = D_SSM // LANES
    n_chunks = N_STATE // SCAN_LANES

    u_pair = jnp.where(s < n_ctx_pairs, uc_ref[...], ul_ref[...])

    def input_proj(u_tile, bu_scr):
        for b in range(SSM_BATCH):
            for c in range(n_slabs):
                ut_scr[c, pl.ds(b, steps, stride=SSM_BATCH), :] = u_tile[b, :, LANES * c:LANES * (c + 1)]
        ub = jnp.concatenate([ut_scr[c] for c in range(n_slabs)], axis=1).astype(BF16)
        for k in range(2):
            r = jnp.dot(ub[:, 256 * k:256 * (k + 1)], bm_ref[k], preferred_element_type=F32)
            bu_scr[:, half_w * k:half_w * (k + 1)] = r[:, :half_w]
            bu_scr[:, N_STATE + half_w * k:N_STATE + half_w * (k + 1)] = r[:, half_w:]

    def scan(bu_scr, h_scr, state, forward):
        out = []
        for c in range(n_chunks):
            re = slice(SCAN_LANES * c, SCAN_LANES * (c + 1))
            im = slice(N_STATE + SCAN_LANES * c, N_STATE + SCAN_LANES * (c + 1))
            a_r = ar_ref[:, re]
            a_i = ai_ref[:, re]
            hr, hi = state[c]
            held = None
            for t in (range(steps) if forward else range(steps - 1, -1, -1)):
                br = bu_scr[t * SSM_BATCH:(t + 1) * SSM_BATCH, re]
                bi = bu_scr[t * SSM_BATCH:(t + 1) * SSM_BATCH, im]
                hr, hi = a_r * hr - a_i * hi + br, a_r * hi + a_i * hr + bi
                if held is None:
                    held = (hr, hi)
                else:
                    (top_r, top_i), (bot_r, bot_i) = (held, (hr, hi)) if forward else ((hr, hi), held)
                    r0 = min(t, t + (-1 if forward else 1)) * SSM_BATCH
                    h_scr[r0:r0 + 2 * SSM_BATCH, re] = jnp.concatenate([top_r, bot_r], axis=0).astype(BF16)
                    h_scr[r0:r0 + 2 * SSM_BATCH, im] = jnp.concatenate([top_i, bot_i], axis=0).astype(BF16)
                    held = None
            out.append((hr, hi))
        return out

    def readout(h_scr, t_lo):
        for k in range(2):
            h_re = h_scr[:, half_w * k:half_w * (k + 1)]
            h_im = h_scr[:, N_STATE + half_w * k:N_STATE + half_w * (k + 1)]
            y = (jnp.dot(h_re, cm_ref[k, 0], preferred_element_type=F32)
                 + jnp.dot(h_im, cm_ref[k, 1], preferred_element_type=F32))
            for c in range(2):
                yt_scr[2 * k + c] = y[:, LANES * c:LANES * (c + 1)]
        for b in range(SSM_BATCH):
            y_ref[b, t_lo:t_lo + steps, :] = jnp.concatenate(
                [yt_scr[c, pl.ds(b, steps, stride=SSM_BATCH), :] for c in range(n_slabs)], axis=1)

    def load_state():
        return [(hr_scr[:, SCAN_LANES * c:SCAN_LANES * (c + 1)], hi_scr[:, SCAN_LANES * c:SCAN_LANES * (c + 1)])
                for c in range(n_chunks)]

    def save_state(state):
        for c, (hr, hi) in enumerate(state):
            hr_scr[:, SCAN_LANES * c:SCAN_LANES * (c + 1)] = hr
            hi_scr[:, SCAN_LANES * c:SCAN_LANES * (c + 1)] = hi

    def run(forward):
        lo_a, lo_b = (0, steps) if forward else (steps, 0)
        u_a = u_pair[:, lo_a:lo_a + steps, :]
        u_b = u_pair[:, lo_b:lo_b + steps, :]

        @pl.when(s == 0)
        def _():
            hr_scr[...] = jnp.zeros_like(hr_scr)
            hi_scr[...] = jnp.zeros_like(hi_scr)
            input_proj(u_a, bu_a)

        @pl.when(s == 0)
        def _():
            save_state(scan(bu_a, h_a, load_state(), forward))
            input_proj(u_b, bu_b)

        @pl.when(s > 0)
        def _():
            save_state(scan(bu_b, h_b, load_state(), forward))
            readout(h_a, lo_a)
            input_proj(u_a, bu_a)

        @pl.when(s > 0)
        def _():
            save_state(scan(bu_a, h_a, load_state(), forward))
            readout(h_b, lo_b)
            input_proj(u_b, bu_b)

    @pl.when(d == 0)
    def _():
        run(True)

    @pl.when(d == 1)
    def _():
        run(False)


def _ssm(u_c, u_l, bm, cm, a_re, a_im):
    bsz, n_ctx, _ = u_c.shape
    n = u_l.shape[1]
    t = SSM_STEPS
    ncp = n_ctx // (2 * t)
    nlp = n // (2 * t)
    rows = t * SSM_BATCH

    def ctx_in(d, s):
        p = jnp.minimum(s, ncp - 1)
        return jnp.where(d == 0, p, ncp - 1 - p)

    def lat_block(d, j):
        return jnp.where(d == 0, j, nlp - 1 - j)

    def lat_in(d, s):
        return lat_block(d, jnp.clip(s - ncp, 0, nlp - 1))

    def lat_out(d, s):
        return lat_block(d, jnp.clip(s - 1 - ncp, 0, nlp - 1))

    once = pl.Buffered(1)
    return pl.pallas_call(
        functools.partial(_ssm_body, n_ctx_pairs=ncp),
        grid=(2, bsz // SSM_BATCH, ncp + nlp + 1),
        in_specs=[pl.BlockSpec((SSM_BATCH, 2 * t, D_SSM), lambda d, bh, s: (bh, ctx_in(d, s), 0)),
                  pl.BlockSpec((SSM_BATCH, 2 * t, D_SSM), lambda d, bh, s: (bh, lat_in(d, s), 0)),
                  pl.BlockSpec((None, 2, 256, N_STATE), lambda d, bh, s: (d, 0, 0, 0), pipeline_mode=once),
                  pl.BlockSpec((None, 2, 2, N_STATE // 2, 256), lambda d, bh, s: (d, 0, 0, 0, 0),
                               pipeline_mode=once),
                  pl.BlockSpec((None, SUBLANES, N_STATE), lambda d, bh, s: (d, 0, 0), pipeline_mode=once),
                  pl.BlockSpec((None, SUBLANES, N_STATE), lambda d, bh, s: (d, 0, 0), pipeline_mode=once)],
        out_specs=pl.BlockSpec((None, SSM_BATCH, 2 * t, D_SSM), lambda d, bh, s: (d, bh, lat_out(d, s), 0)),
        out_shape=jax.ShapeDtypeStruct((2, bsz, n, D_SSM), F32),
        scratch_shapes=[pltpu.VMEM((D_SSM // LANES, rows, LANES), F32),
                        pltpu.VMEM((D_SSM // LANES, rows, LANES), F32),
                        pltpu.VMEM((rows, 2 * N_STATE), F32),
                        pltpu.VMEM((rows, 2 * N_STATE), F32),
                        pltpu.VMEM((rows, 2 * N_STATE), BF16),
                        pltpu.VMEM((rows, 2 * N_STATE), BF16),
                        pltpu.VMEM((SSM_BATCH, N_STATE), F32),
                        pltpu.VMEM((SSM_BATCH, N_STATE), F32)],
        compiler_params=_params(3),
        name="ssm",
    )(u_c, u_l, bm, cm, a_re, a_im)


def _merge_body(x_ref, yf_ref, yb_ref, u_ref, at_ref,
                sh1_ref, sc1_ref, g1_ref, sh2_ref, sc2_ref, g2_ref, n1_ref, n2_ref,
                wg_ref, dv_ref, wglu_ref, bglu_ref, wba_ref, wbs_ref, wo_ref,
                w1_ref, b1_ref, w2_ref, b2_ref, o_ref):
    x = x_ref[...]
    hb = _rms_modulate(x, n1_ref[...], sc1_ref[...], sh1_ref[...]).astype(BF16)
    gates = jnp.dot(hb, wg_ref[...], preferred_element_type=F32)
    gate_a = jax.nn.sigmoid(gates[:, :D_MODEL])
    gate_s = jax.nn.sigmoid(gates[:, D_MODEL:])

    y = yf_ref[...] + yb_ref[...] + dv_ref[...] * u_ref[...]
    y = jax.nn.gelu(y)
    z = jnp.dot(y.astype(BF16), wglu_ref[...], preferred_element_type=F32) + bglu_ref[...]
    y = y * jax.nn.sigmoid(z)

    merged = (gate_a * jnp.dot(at_ref[...], wba_ref[...], preferred_element_type=F32)
              + gate_s * jnp.dot(y.astype(BF16), wbs_ref[...], preferred_element_type=F32))
    mix = jnp.dot(merged.astype(BF16), wo_ref[...], preferred_element_type=F32)
    x1 = x + g1_ref[...] * mix

    h2 = _rms_modulate(x1, n2_ref[...], sc2_ref[...], sh2_ref[...]).astype(BF16)
    acc = jnp.zeros_like(x1)
    for c in range(D_FF // FF_CHUNK):
        cols = slice(FF_CHUNK * c, FF_CHUNK * (c + 1))
        hid = jnp.dot(h2, w1_ref[:, cols], preferred_element_type=F32) + b1_ref[:, cols]
        hid = jnp.square(jnp.maximum(hid, 0.0))
        acc = acc + jnp.dot(hid.astype(BF16), w2_ref[cols, :], preferred_element_type=F32)
    o_ref[...] = x1 + g2_ref[...] * (acc + b2_ref[...])


def _merge(x, y2, u2, attn, mods, n1, n2, wg, dvec, wglu, bglu, wba, wbs, wo, w1, b1, w2, b2):
    bsz, n, _ = x.shape
    tm = MERGE_ROWS
    row_spec = pl.BlockSpec((None, tm, D_MODEL), lambda b, i: (b, i, 0))
    mod_spec = pl.BlockSpec((None, 1, D_MODEL), lambda b, i: (b, 0, 0))
    in_specs = [row_spec,
                pl.BlockSpec((None, None, tm, D_SSM), lambda b, i: (0, b, i, 0)),
                pl.BlockSpec((None, None, tm, D_SSM), lambda b, i: (1, b, i, 0)),
                pl.BlockSpec((None, tm, D_SSM), lambda b, i: (b, i, 0)),
                row_spec] + [mod_spec] * 6
    consts = [n1, n2, wg, dvec, wglu, bglu, wba, wbs, wo, w1, b1, w2, b2]
    in_specs += [_const_spec(a.shape) for a in consts]
    return pl.pallas_call(
        _merge_body,
        grid=(bsz, n // tm),
        in_specs=in_specs,
        out_specs=row_spec,
        out_shape=jax.ShapeDtypeStruct(x.shape, x.dtype),
        compiler_params=_params(2),
        name="merge",
    )(x, y2, y2, u2, attn, *mods, *consts)


def _rope_tables(n_tok):
    half = ROPE_AXIS_DIM // 2
    pos = jnp.arange(n_tok, dtype=jnp.int32)
    row = (pos // GRID_W).astype(F32)
    col = (pos % GRID_W).astype(F32)
    inv_freq = ROPE_THETA ** (-jnp.arange(half, dtype=F32) / half)
    ang_r = row[:, None] * inv_freq
    ang_c = col[:, None] * inv_freq
    cos = jnp.concatenate([jnp.cos(ang_r)] * 2 + [jnp.cos(ang_c)] * 2, axis=-1)
    sin = jnp.concatenate([-jnp.sin(ang_r), jnp.sin(ang_r), -jnp.sin(ang_c), jnp.sin(ang_c)], axis=-1)
    reps = LANES // HEAD_DIM
    return jnp.tile(cos, (1, reps)), jnp.tile(sin, (1, reps))


def kernel(x, c, ctx, c_ctx, w_mod, b_mod, norm1_g, norm2_g, w_in, q_norm_g, k_norm_g,
           ssm_lambda_re, ssm_lambda_im, ssm_log_dt, ssm_b_re, ssm_b_im, ssm_c_re, ssm_c_im,
           ssm_d, w_glu, b_glu, w_br_attn, w_br_ssm, w_out, w_mlp1, b_mlp1, w_mlp2, b_mlp2):
    bsz, n_tok, d_model = x.shape
    n_ctx = ctx.shape[1]
    assert d_model == D_MODEL and w_mod.shape[0] == 1, "single-layer block of width 1024 only"
    assert n_tok % PROJ_ROWS == 0 and n_ctx % PROJ_ROWS == 0 and n_ctx % SSM_STEPS == 0
    assert bsz % SSM_BATCH == 0 and n_tok % GRID_W == 0

    pad = (-(bsz + 1)) % SUBLANES
    c_rows = jnp.concatenate([c, c_ctx[None, :], jnp.zeros((pad, D_MODEL), F32)], axis=0)
    mod = _modulation(c_rows, w_mod[0], b_mod[0][None, :])
    sh1, sc1, g1, sh2, sc2, g2 = [m[:, None, :] for m in jnp.split(mod, 6, axis=-1)]
    lat = lambda m: m[:bsz]
    cx = lambda m: m[bsz:bsz + 1]

    w = w_in[0]
    w_k = w[:, OFF_K:OFF_V].reshape(D_MODEL, N_KV_HEADS, 1, HEAD_DIM)
    w_kdup = jnp.broadcast_to(w_k, (D_MODEL, N_KV_HEADS, 2, HEAD_DIM)).reshape(D_MODEL, 2 * OFF_V - 2 * OFF_K)
    w_u = w[:, OFF_U:OFF_GA]
    wm_lat = jnp.concatenate([w[:, :OFF_K], w_kdup, w_u], axis=1).astype(BF16)
    wm_ctx = jnp.concatenate([w_kdup, w_u], axis=1).astype(BF16)
    w_vt = w[:, OFF_V:OFF_U].T.astype(BF16)
    w_gate = w[:, OFF_GA:].astype(BF16)
    mavg = jnp.kron(jnp.eye(MXU_TILE // HEAD_DIM, dtype=F32),
                    jnp.full((HEAD_DIM, HEAD_DIM), 1.0 / HEAD_DIM, F32)).astype(BF16)
    gk = jnp.tile(k_norm_g[0], 2 * N_KV_HEADS)
    gq = jnp.tile(q_norm_g[0], N_HEADS) * (HEAD_DIM ** -0.5 * LOG2_E)
    gn_lat = jnp.concatenate([gq, gk])[None, :]
    gn_ctx = gk[None, :]
    cos, sin = _rope_tables(n_tok)
    n1 = norm1_g[0][None, :]

    q, k_l, vt_l, u_l = _project(x, lat(sh1), lat(sc1), n1, wm_lat, w_vt, mavg, gn_lat, cos, sin)
    _, k_c, vt_c, u_c = _project(ctx, cx(sh1), cx(sc1), n1, wm_ctx, w_vt, mavg, gn_ctx, None, None)

    attn = _attention(q, k_l, k_c, vt_l, vt_c)

    a_re, a_im, bb_re, bb_im = _discretize(ssm_lambda_re[0], ssm_lambda_im[0], ssm_log_dt[0],
                                           ssm_b_re[0], ssm_b_im[0])
    bm, cm, a_re8, a_im8 = _ssm_matrices(a_re, a_im, bb_re, bb_im, ssm_c_re[0], ssm_c_im[0])
    y = _ssm(u_c, u_l, bm, cm, a_re8, a_im8)

    mods = [lat(m) for m in (sh1, sc1, g1, sh2, sc2, g2)]
    return _merge(x, y, u_l, attn, mods, n1, norm2_g[0][None, :],
                  w_gate, ssm_d[0][None, :], w_glu[0].astype(BF16), b_glu[0][None, :],
                  w_br_attn[0].astype(BF16), w_br_ssm[0].astype(BF16), w_out[0].astype(BF16),
                  w_mlp1[0].astype(BF16), b_mlp1[0][None, :], w_mlp2[0].astype(BF16), b_mlp2[0][None, :])
```

```python
import functools

import jax
import jax.numpy as jnp
from jax import lax
from jax.experimental import pallas as pl
from jax.experimental.pallas import tpu as pltpu

F32 = jnp.float32
BF16 = jnp.bfloat16

D_MODEL = 1024
N_HEADS = 16
N_KV_HEADS = 4
HEAD_DIM = 64
KV_REP = N_HEADS // N_KV_HEADS
GRID_W = 64
ROPE_THETA = 10000.0
ROPE_AXIS_DIM = HEAD_DIM // 2
D_SSM = 512
SSM_GROUP = 16
N_SSM_GROUPS = 32
SSM_STATE = 64
N_STATE = N_SSM_GROUPS * SSM_STATE
D_FF = 4 * D_MODEL
NORM_EPS = 1e-6
OFF_K = 1024
OFF_V = 1280
OFF_U = 1536
OFF_GA = 2048

LANES = 128
SUBLANES = 8
MXU_TILE = 256
LOG2_E = 1.4426950408889634
VMEM_LIMIT_BYTES = 56 * 1024 * 1024

PROJ_ROWS = 512
PROJ_SUB_ROWS = 256
ATTN_ROWS = 256
ATTN_KEYS = 256
ONES_ROWS = 16
MERGE_ROWS = 256
SSM_STEPS = 128
SSM_BATCH = SUBLANES
SCAN_LANES = 1024
SCAN_UNROLL = 8
FF_CHUNK = 1024


def _const_spec(shape):
    nd = len(shape)
    return pl.BlockSpec(shape, lambda *_: (0,) * nd, pipeline_mode=pl.Buffered(1))


def _params(n_axes, vmem=VMEM_LIMIT_BYTES, flags=None):
    return pltpu.CompilerParams(dimension_semantics=("arbitrary",) * n_axes, vmem_limit_bytes=vmem,
                                flags=flags)


def _mod_body(c_ref, w_ref, b_ref, o_ref):
    c = c_ref[...]
    s = c * jax.nn.sigmoid(c)
    o_ref[...] = jnp.dot(s, w_ref[...], preferred_element_type=F32) + b_ref[...]


def _modulation(c_rows, w_mod, b_mod):
    rows = c_rows.shape[0]
    ncol = w_mod.shape[1]
    blk = D_MODEL
    return pl.pallas_call(
        _mod_body,
        grid=(ncol // blk,),
        in_specs=[pl.BlockSpec((rows, D_MODEL), lambda j: (0, 0)),
                  pl.BlockSpec((D_MODEL, blk), lambda j: (0, j)),
                  pl.BlockSpec((1, blk), lambda j: (0, j))],
        out_specs=pl.BlockSpec((rows, blk), lambda j: (0, j)),
        out_shape=jax.ShapeDtypeStruct((rows, ncol), F32),
        compiler_params=_params(1),
        name="mod",
    )(c_rows, w_mod, b_mod)


def _disc_body(lr_ref, li_ref, ldt_ref, br_ref, bi_ref, ar_ref, ai_ref, bbr_ref, bbi_ref):
    lr = lr_ref[...]
    li = li_ref[...]
    dt = jnp.exp(ldt_ref[...])
    mag = jnp.exp(lr * dt)
    ar = mag * jnp.cos(li * dt)
    ai = mag * jnp.sin(li * dt)
    nr = ar - 1.0
    den = lr * lr + li * li
    cr = (nr * lr + ai * li) / den
    ci = (ai * lr - nr * li) / den
    br = br_ref[...]
    bi = bi_ref[...]
    ar_ref[...] = ar
    ai_ref[...] = ai
    bbr_ref[...] = cr * br - ci * bi
    bbi_ref[...] = cr * bi + ci * br


def _discretize(lam_re, lam_im, log_dt, b_re, b_im):
    rows = 2 * N_SSM_GROUPS
    cols = SSM_STATE * SSM_GROUP
    rep = lambda a: jnp.repeat(a.reshape(rows, SSM_STATE), SSM_GROUP, axis=1)
    ldt = jnp.broadcast_to(log_dt.reshape(rows, 1), (rows, cols))
    spec = pl.BlockSpec((rows, cols), lambda: (0, 0))
    sds = jax.ShapeDtypeStruct((rows, cols), F32)
    ar, ai, bbr, bbi = pl.pallas_call(
        _disc_body,
        in_specs=[spec] * 5,
        out_specs=[spec] * 4,
        out_shape=[sds] * 4,
        name="disc",
    )(rep(lam_re), rep(lam_im), ldt, b_re.reshape(rows, cols), b_im.reshape(rows, cols))
    a_re = ar[:, ::SSM_GROUP].reshape(2, N_STATE)
    a_im = ai[:, ::SSM_GROUP].reshape(2, N_STATE)
    shape = (2, N_SSM_GROUPS, SSM_STATE, SSM_GROUP)
    return a_re, a_im, bbr.reshape(shape), bbi.reshape(shape)


def _ssm_matrices(a_re, a_im, bb_re, bb_im, c_re, c_im):
    eye = jnp.eye(16, dtype=F32)

    def in_blocks(bb):
        t = bb.reshape(2, 2, 16, SSM_STATE, SSM_GROUP).transpose(0, 1, 2, 4, 3)
        return jnp.einsum('dkghp,gG->dkghGp', t, eye).reshape(2, 2, 256, 1024)

    def out_blocks(c):
        t = c.reshape(2, 2, 16, SSM_GROUP, SSM_STATE).transpose(0, 1, 2, 4, 3)
        return jnp.einsum('dkgph,gG->dkgpGh', t, eye).reshape(2, 2, 1024, 256)

    bm = jnp.concatenate([in_blocks(bb_re), in_blocks(bb_im)], axis=-1).astype(BF16)
    cm = jnp.stack([out_blocks(c_re), out_blocks(-c_im)], axis=2).astype(BF16)
    bcast = lambda a: jnp.broadcast_to(a[:, None, :], (2, SUBLANES, N_STATE))
    return bm, cm, bcast(a_re), bcast(a_im)


def _rms_modulate(x, g, sc, sh):
    ms = jnp.mean(x * x, axis=-1, keepdims=True)
    xn = x * lax.rsqrt(ms + NORM_EPS) * g
    return xn * (1.0 + sc) + sh


def _proj_body(*refs, n_norm, rope):
    if rope:
        (x_ref, sh_ref, sc_ref, g_ref, wm_ref, wvt_ref, mavg_ref, gn_ref, cos_ref, sin_ref,
         q_ref, k_ref, vt_ref, u_ref) = refs
    else:
        (x_ref, sh_ref, sc_ref, g_ref, wm_ref, wvt_ref, mavg_ref, gn_ref,
         k_ref, vt_ref, u_ref) = refs
    rows = x_ref.shape[0]
    sub = min(rows, PROJ_SUB_ROWS)
    mavg = mavg_ref[...]
    width = mavg.shape[0]
    n_q_chunks = n_norm // LANES - k_ref.shape[0]
    if rope:
        lane = lax.broadcasted_iota(jnp.int32, (sub, LANES), 1)
        half = ROPE_AXIS_DIM // 2
        first_half = (lane % ROPE_AXIS_DIM) < half
    for r0 in range(0, rows, sub):
        rs = slice(r0, r0 + sub)
        h = _rms_modulate(x_ref[rs, :], g_ref[...], sc_ref[...], sh_ref[...])
        hb = h.astype(BF16)
        p = jnp.dot(hb, wm_ref[...], preferred_element_type=F32)
        qk = p[:, :n_norm]
        u_ref[rs, :] = p[:, n_norm:]
        sq = (qk * qk).astype(BF16)
        msq = jnp.concatenate(
            [jnp.dot(sq[:, width * j:width * (j + 1)], mavg, preferred_element_type=F32)
             for j in range(n_norm // width)], axis=1)
        vt = lax.dot_general(wvt_ref[...], hb, (((1,), (1,)), ((), ())), preferred_element_type=F32)
        for g in range(N_KV_HEADS):
            vt_ref[g, :, rs] = vt[HEAD_DIM * g:HEAD_DIM * (g + 1), :].astype(BF16)
        qkn = qk * lax.rsqrt(msq + NORM_EPS) * gn_ref[...]
        for c in range(n_norm // LANES):
            xc = qkn[:, LANES * c:LANES * (c + 1)]
            if rope:
                partner = jnp.where(first_half, pltpu.roll(xc, LANES - half, 1), pltpu.roll(xc, half, 1))
                xc = xc * cos_ref[rs, :] + partner * sin_ref[rs, :]
            if c < n_q_chunks:
                q_ref[c, rs, :] = xc.astype(BF16)
            else:
                k_ref[c - n_q_chunks, rs, :] = xc.astype(BF16)


def _project(x, sh, sc, g, wm, wvt, mavg, gn, cos, sin):
    bsz, n, _ = x.shape
    tm = min(PROJ_ROWS, n)
    rope = cos is not None
    n_norm = gn.shape[1]
    ncol = wm.shape[1]
    kv_w = N_KV_HEADS * HEAD_DIM
    q_chunks = D_MODEL // LANES
    mod_rows = sh.shape[0]
    mod_spec = pl.BlockSpec((None, 1, D_MODEL),
                            (lambda i, b: (b, 0, 0)) if mod_rows == bsz else (lambda i, b: (0, 0, 0)))
    in_specs = [pl.BlockSpec((None, tm, D_MODEL), lambda i, b: (b, i, 0)),
                mod_spec, mod_spec,
                _const_spec((1, D_MODEL)),
                _const_spec((D_MODEL, ncol)),
                _const_spec((kv_w, D_MODEL)),
                _const_spec(mavg.shape),
                _const_spec((1, n_norm))]
    args = [x, sh, sc, g, wm, wvt, mavg, gn]
    out_specs = [pl.BlockSpec((None, N_KV_HEADS, tm, LANES), lambda i, b: (b, 0, i, 0)),
                 pl.BlockSpec((None, N_KV_HEADS, HEAD_DIM, tm), lambda i, b: (b, 0, 0, i)),
                 pl.BlockSpec((None, tm, D_SSM), lambda i, b: (b, i, 0))]
    out_shape = [jax.ShapeDtypeStruct((bsz, N_KV_HEADS, n, LANES), BF16),
                 jax.ShapeDtypeStruct((bsz, N_KV_HEADS, HEAD_DIM, n), BF16),
                 jax.ShapeDtypeStruct((bsz, n, D_SSM), F32)]
    if rope:
        in_specs += [pl.BlockSpec((tm, LANES), lambda i, b: (i, 0))] * 2
        args += [cos, sin]
        out_specs = [pl.BlockSpec((None, q_chunks, tm, LANES), lambda i, b: (b, 0, i, 0))] + out_specs
        out_shape = [jax.ShapeDtypeStruct((bsz, q_chunks, n, LANES), BF16)] + out_shape
    outs = pl.pallas_call(
        functools.partial(_proj_body, n_norm=n_norm, rope=rope),
        grid=(n // tm, bsz),
        in_specs=in_specs,
        out_specs=out_specs,
        out_shape=out_shape,
        compiler_params=_params(2),
        name="proj_latent" if rope else "proj_ctx",
    )(*args)
    return outs if rope else [None] + list(outs)


def _attn_body(q_ref, kl_ref, kc_ref, vtl_ref, vtc_ref, o_ref, s0_scr, s1_scr, m_scr, o_scr):
    n_ctx = kc_ref.shape[1]
    n_lat = kl_ref.shape[1]
    n_blocks = (n_ctx + n_lat) // ATTN_KEYS
    tq = q_ref.shape[1]
    lane = lax.broadcasted_iota(jnp.int32, (tq, LANES), 1)
    head_lanes = (lane < HEAD_DIM, lane >= HEAD_DIM)
    s_bufs = (s0_scr, s1_scr)
    nt = (((1,), (1,)), ((), ()))
    reload_base = jnp.minimum(pl.program_id(1), 0)

    def slab_max(s):
        return jnp.max(s.reshape(s.shape[0] // SUBLANES, SUBLANES, tq), axis=0)

    def scores(g, r, slot):
        s_scr = s_bufs[slot]
        qc = q_ref[2 * g + r // 2]
        qm = jnp.where(head_lanes[r % 2], qc, jnp.zeros_like(qc))
        sc = lax.dot_general(kc_ref[g], qm, nt, preferred_element_type=F32)
        sl = lax.dot_general(kl_ref[g], qm, nt, preferred_element_type=F32)
        s_scr[:n_ctx, :] = sc
        s_scr[n_ctx:, :] = sl
        m_scr[slot] = jnp.maximum(slab_max(sc), slab_max(sl))

    ones_rows = jnp.ones((ONES_ROWS, ATTN_KEYS), BF16)

    def finish(g, r, slot):
        s_scr = s_bufs[slot]
        m = jnp.max(m_scr[slot], axis=0, keepdims=True)
        m8 = jnp.broadcast_to(m, (SUBLANES, tq))
        acc = jnp.zeros((HEAD_DIM + ONES_ROWS, tq), F32)
        for kb in range(n_blocks):
            lo = kb * ATTN_KEYS
            row = pl.multiple_of(reload_base + lo, ATTN_KEYS)
            s = s_scr[pl.ds(row, ATTN_KEYS), :].reshape(ATTN_KEYS // SUBLANES, SUBLANES, tq)
            pb = jnp.exp2(s - m8).reshape(ATTN_KEYS, tq).astype(BF16)
            if lo < n_ctx:
                vt = vtc_ref[g, :, lo:lo + ATTN_KEYS]
            else:
                vt = vtl_ref[g, :, lo - n_ctx:lo - n_ctx + ATTN_KEYS]
            acc = acc + jnp.dot(jnp.concatenate([vt, ones_rows], axis=0), pb, preferred_element_type=F32)
        o_scr[KV_REP * g + r] = acc

    scores(0, 0, 0)

    def group(g, carry):
        for r in range(KV_REP):
            if r + 1 < KV_REP:
                scores(g, r + 1, (r + 1) % 2)
            else:
                scores(jnp.minimum(g + 1, N_KV_HEADS - 1), 0, (r + 1) % 2)
            finish(g, r, r % 2)
        return carry

    lax.fori_loop(0, N_KV_HEADS, group, 0)
    o_all = o_scr[...]
    o_all = o_all[:, :HEAD_DIM, :] / o_all[:, HEAD_DIM:HEAD_DIM + 1, :]
    o_ref[...] = o_all.reshape(N_HEADS * HEAD_DIM, tq).T.astype(BF16)


def _attention(q, k_l, k_c, vt_l, vt_c):
    bsz, q_chunks, n, _ = q.shape
    n_ctx = k_c.shape[2]
    tq = ATTN_ROWS
    whole = lambda shape: pl.BlockSpec((None,) + shape, lambda b, j: (b,) + (0,) * len(shape))
    return pl.pallas_call(
        _attn_body,
        grid=(bsz, n // tq),
        in_specs=[pl.BlockSpec((None, q_chunks, tq, LANES), lambda b, j: (b, 0, j, 0)),
                  whole((N_KV_HEADS, n, LANES)),
                  whole((N_KV_HEADS, n_ctx, LANES)),
                  whole((N_KV_HEADS, HEAD_DIM, n)),
                  whole((N_KV_HEADS, HEAD_DIM, n_ctx))],
        out_specs=pl.BlockSpec((None, tq, D_MODEL), lambda b, j: (b, j, 0)),
        out_shape=jax.ShapeDtypeStruct((bsz, n, D_MODEL), BF16),
        scratch_shapes=[pltpu.VMEM((n_ctx + n, tq), F32),
                        pltpu.VMEM((n_ctx + n, tq), F32),
                        pltpu.VMEM((2, SUBLANES, tq), F32),
                        pltpu.VMEM((N_HEADS, HEAD_DIM + ONES_ROWS, tq), F32)],
        compiler_params=_params(2),
        name="attn",
    )(q, k_l, k_c, vt_l, vt_c)


def _ssm_body(uc_ref, ul_ref, bm_ref, cm_ref, ar_ref, ai_ref, y_ref,
              ut_scr, yt_scr, st_scr, hr_scr, hi_scr, *, n_ctx_tiles):
    d = pl.program_id(0)
    i = pl.program_id(2)
    steps = uc_ref.shape[1]
    half_w = N_STATE // 2
    n_slabs = D_SSM // LANES

    @pl.when(i == 0)
    def _():
        hr_scr[...] = jnp.zeros_like(hr_scr)
        hi_scr[...] = jnp.zeros_like(hi_scr)

    u = jnp.where(i < n_ctx_tiles, uc_ref[...], ul_ref[...])
    for b in range(SSM_BATCH):
        for c in range(n_slabs):
            ut_scr[c, pl.ds(b, steps, stride=SSM_BATCH), :] = u[b, :, LANES * c:LANES * (c + 1)]
    ub = jnp.concatenate([ut_scr[c] for c in range(n_slabs)], axis=1).astype(BF16)
    for k in range(2):
        r = jnp.dot(ub[:, 256 * k:256 * (k + 1)], bm_ref[k], preferred_element_type=F32)
        st_scr[:, half_w * k:half_w * (k + 1)] = r[:, :half_w]
        st_scr[:, N_STATE + half_w * k:N_STATE + half_w * (k + 1)] = r[:, half_w:]

    def scan(forward):
        for c in range(N_STATE // SCAN_LANES):
            re = slice(SCAN_LANES * c, SCAN_LANES * (c + 1))
            im = slice(N_STATE + SCAN_LANES * c, N_STATE + SCAN_LANES * (c + 1))
            a_r = ar_ref[:, re]
            a_i = ai_ref[:, re]

            def step(j, carry):
                hr, hi = carry
                t = j if forward else steps - 1 - j
                row = pl.multiple_of(t * SSM_BATCH, SSM_BATCH)
                br = st_scr[pl.ds(row, SSM_BATCH), re]
                bi = st_scr[pl.ds(row, SSM_BATCH), im]
                hr, hi = a_r * hr - a_i * hi + br, a_r * hi + a_i * hr + bi
                st_scr[pl.ds(row, SSM_BATCH), re] = hr
                st_scr[pl.ds(row, SSM_BATCH), im] = hi
                return hr, hi

            hr, hi = lax.fori_loop(0, steps, step, (hr_scr[:, re], hi_scr[:, re]), unroll=SCAN_UNROLL)
            hr_scr[:, re] = hr
            hi_scr[:, re] = hi

    @pl.when(d == 0)
    def _():
        scan(True)

    @pl.when(d == 1)
    def _():
        scan(False)

    @pl.when(i >= n_ctx_tiles)
    def _():
        for k in range(2):
            y = (jnp.dot(st_scr[:, half_w * k:half_w * (k + 1)].astype(BF16), cm_ref[k, 0],
                         preferred_element_type=F32)
                 + jnp.dot(st_scr[:, N_STATE + half_w * k:N_STATE + half_w * (k + 1)].astype(BF16), cm_ref[k, 1],
                           preferred_element_type=F32))
            for c in range(2):
                yt_scr[2 * k + c] = y[:, LANES * c:LANES * (c + 1)]
        for b in range(SSM_BATCH):
            y_ref[b] = jnp.concatenate(
                [yt_scr[c, pl.ds(b, steps, stride=SSM_BATCH), :] for c in range(n_slabs)], axis=1)


def _ssm(u_c, u_l, bm, cm, a_re, a_im):
    bsz, n_ctx, _ = u_c.shape
    n = u_l.shape[1]
    t = SSM_STEPS
    nct = n_ctx // t
    nlt = n // t
    rows = t * SSM_BATCH

    def ctx_tile(d, i):
        return jnp.where(d == 0, jnp.minimum(i, nct - 1), jnp.maximum(nct - 1 - i, 0))

    def lat_tile(d, i):
        j = jnp.maximum(i - nct, 0)
        return jnp.where(d == 0, j, nlt - 1 - j)

    once = pl.Buffered(1)
    return pl.pallas_call(
        functools.partial(_ssm_body, n_ctx_tiles=nct),
        grid=(2, bsz // SSM_BATCH, nct + nlt),
        in_specs=[pl.BlockSpec((SSM_BATCH, t, D_SSM), lambda d, bh, i: (bh, ctx_tile(d, i), 0)),
                  pl.BlockSpec((SSM_BATCH, t, D_SSM), lambda d, bh, i: (bh, lat_tile(d, i), 0)),
                  pl.BlockSpec((None, 2, 256, N_STATE), lambda d, bh, i: (d, 0, 0, 0), pipeline_mode=once),
                  pl.BlockSpec((None, 2, 2, N_STATE // 2, 256), lambda d, bh, i: (d, 0, 0, 0, 0),
                               pipeline_mode=once),
                  pl.BlockSpec((None, SUBLANES, N_STATE), lambda d, bh, i: (d, 0, 0), pipeline_mode=once),
                  pl.BlockSpec((None, SUBLANES, N_STATE), lambda d, bh, i: (d, 0, 0), pipeline_mode=once)],
        out_specs=pl.BlockSpec((None, SSM_BATCH, t, D_SSM), lambda d, bh, i: (d, bh, lat_tile(d, i), 0)),
        out_shape=jax.ShapeDtypeStruct((2, bsz, n, D_SSM), F32),
        scratch_shapes=[pltpu.VMEM((D_SSM // LANES, rows, LANES), F32),
                        pltpu.VMEM((D_SSM // LANES, rows, LANES), F32),
                        pltpu.VMEM((rows, 2 * N_STATE), F32),
                        pltpu.VMEM((SSM_BATCH, N_STATE), F32),
                        pltpu.VMEM((SSM_BATCH, N_STATE), F32)],
        compiler_params=_params(3),
        name="ssm",
    )(u_c, u_l, bm, cm, a_re, a_im)


def _merge_body(x_ref, yf_ref, yb_ref, u_ref, at_ref,
                sh1_ref, sc1_ref, g1_ref, sh2_ref, sc2_ref, g2_ref, n1_ref, n2_ref,
                wg_ref, dv_ref, wglu_ref, bglu_ref, wba_ref, wbs_ref, wo_ref,
                w1_ref, b1_ref, w2_ref, b2_ref, o_ref):
    x = x_ref[...]
    hb = _rms_modulate(x, n1_ref[...], sc1_ref[...], sh1_ref[...]).astype(BF16)
    gates = jnp.dot(hb, wg_ref[...], preferred_element_type=F32)
    gate_a = jax.nn.sigmoid(gates[:, :D_MODEL])
    gate_s = jax.nn.sigmoid(gates[:, D_MODEL:])

    y = yf_ref[...] + yb_ref[...] + dv_ref[...] * u_ref[...]
    y = jax.nn.gelu(y)
    z = jnp.dot(y.astype(BF16), wglu_ref[...], preferred_element_type=F32) + bglu_ref[...]
    y = y * jax.nn.sigmoid(z)

    merged = (gate_a * jnp.dot(at_ref[...], wba_ref[...], preferred_element_type=F32)
              + gate_s * jnp.dot(y.astype(BF16), wbs_ref[...], preferred_element_type=F32))
    mix = jnp.dot(merged.astype(BF16), wo_ref[...], preferred_element_type=F32)
    x1 = x + g1_ref[...] * mix

    h2 = _rms_modulate(x1, n2_ref[...], sc2_ref[...], sh2_ref[...]).astype(BF16)
    acc = jnp.zeros_like(x1)
    for c in range(D_FF // FF_CHUNK):
        cols = slice(FF_CHUNK * c, FF_CHUNK * (c + 1))
        hid = jnp.dot(h2, w1_ref[:, cols], preferred_element_type=F32) + b1_ref[:, cols]
        hid = jnp.square(jnp.maximum(hid, 0.0))
        acc = acc + jnp.dot(hid.astype(BF16), w2_ref[cols, :], preferred_element_type=F32)
    o_ref[...] = x1 + g2_ref[...] * (acc + b2_ref[...])


def _merge(x, y2, u2, attn, mods, n1, n2, wg, dvec, wglu, bglu, wba, wbs, wo, w1, b1, w2, b2):
    bsz, n, _ = x.shape
    tm = MERGE_ROWS
    row_spec = pl.BlockSpec((None, tm, D_MODEL), lambda b, i: (b, i, 0))
    mod_spec = pl.BlockSpec((None, 1, D_MODEL), lambda b, i: (b, 0, 0))
    in_specs = [row_spec,
                pl.BlockSpec((None, None, tm, D_SSM), lambda b, i: (0, b, i, 0)),
                pl.BlockSpec((None, None, tm, D_SSM), lambda b, i: (1, b, i, 0)),
                pl.BlockSpec((None, tm, D_SSM), lambda b, i: (b, i, 0)),
                row_spec] + [mod_spec] * 6
    consts = [n1, n2, wg, dvec, wglu, bglu, wba, wbs, wo, w1, b1, w2, b2]
    in_specs += [_const_spec(a.shape) for a in consts]
    return pl.pallas_call(
        _merge_body,
        grid=(bsz, n // tm),
        in_specs=in_specs,
        out_specs=row_spec,
        out_shape=jax.ShapeDtypeStruct(x.shape, x.dtype),
        compiler_params=_params(2),
        name="merge",
    )(x, y2, y2, u2, attn, *mods, *consts)


def _rope_tables(n_tok):
    half = ROPE_AXIS_DIM // 2
    pos = jnp.arange(n_tok, dtype=jnp.int32)
    row = (pos // GRID_W).astype(F32)
    col = (pos % GRID_W).astype(F32)
    inv_freq = ROPE_THETA ** (-jnp.arange(half, dtype=F32) / half)
    ang_r = row[:, None] * inv_freq
    ang_c = col[:, None] * inv_freq
    cos = jnp.concatenate([jnp.cos(ang_r)] * 2 + [jnp.cos(ang_c)] * 2, axis=-1)
    sin = jnp.concatenate([-jnp.sin(ang_r), jnp.sin(ang_r), -jnp.sin(ang_c), jnp.sin(ang_c)], axis=-1)
    reps = LANES // HEAD_DIM
    return jnp.tile(cos, (1, reps)), jnp.tile(sin, (1, reps))


def kernel(x, c, ctx, c_ctx, w_mod, b_mod, norm1_g, norm2_g, w_in, q_norm_g, k_norm_g,
           ssm_lambda_re, ssm_lambda_im, ssm_log_dt, ssm_b_re, ssm_b_im, ssm_c_re, ssm_c_im,
           ssm_d, w_glu, b_glu, w_br_attn, w_br_ssm, w_out, w_mlp1, b_mlp1, w_mlp2, b_mlp2):
    bsz, n_tok, d_model = x.shape
    n_ctx = ctx.shape[1]
    assert d_model == D_MODEL and w_mod.shape[0] == 1, "single-layer block of width 1024 only"
    assert n_tok % PROJ_ROWS == 0 and n_ctx % min(PROJ_ROWS, n_ctx) == 0 and n_ctx % SSM_STEPS == 0
    assert bsz % SSM_BATCH == 0 and n_tok % GRID_W == 0

    pad = (-(bsz + 1)) % SUBLANES
    c_rows = jnp.concatenate([c, c_ctx[None, :], jnp.zeros((pad, D_MODEL), F32)], axis=0)
    mod = _modulation(c_rows, w_mod[0], b_mod[0][None, :])
    sh1, sc1, g1, sh2, sc2, g2 = [m[:, None, :] for m in jnp.split(mod, 6, axis=-1)]
    lat = lambda m: m[:bsz]
    cx = lambda m: m[bsz:bsz + 1]

    w = w_in[0]
    w_k = w[:, OFF_K:OFF_V].reshape(D_MODEL, N_KV_HEADS, 1, HEAD_DIM)
    w_kdup = jnp.broadcast_to(w_k, (D_MODEL, N_KV_HEADS, 2, HEAD_DIM)).reshape(D_MODEL, 2 * OFF_V - 2 * OFF_K)
    w_u = w[:, OFF_U:OFF_GA]
    wm_lat = jnp.concatenate([w[:, :OFF_K], w_kdup, w_u], axis=1).astype(BF16)
    wm_ctx = jnp.concatenate([w_kdup, w_u], axis=1).astype(BF16)
    w_vt = w[:, OFF_V:OFF_U].T.astype(BF16)
    w_gate = w[:, OFF_GA:].astype(BF16)
    mavg = jnp.kron(jnp.eye(MXU_TILE // HEAD_DIM, dtype=F32),
                    jnp.full((HEAD_DIM, HEAD_DIM), 1.0 / HEAD_DIM, F32)).astype(BF16)
    gk = jnp.tile(k_norm_g[0], 2 * N_KV_HEADS)
    gq = jnp.tile(q_norm_g[0], N_HEADS) * (HEAD_DIM ** -0.5 * LOG2_E)
    gn_lat = jnp.concatenate([gq, gk])[None, :]
    gn_ctx = gk[None, :]
    cos, sin = _rope_tables(n_tok)
    n1 = norm1_g[0][None, :]

    q, k_l, vt_l, u_l = _project(x, lat(sh1), lat(sc1), n1, wm_lat, w_vt, mavg, gn_lat, cos, sin)
    _, k_c, vt_c, u_c = _project(ctx, cx(sh1), cx(sc1), n1, wm_ctx, w_vt, mavg, gn_ctx, None, None)

    attn = _attention(q, k_l, k_c, vt_l, vt_c)

    a_re, a_im, bb_re, bb_im = _discretize(ssm_lambda_re[0], ssm_lambda_im[0], ssm_log_dt[0],
                                           ssm_b_re[0], ssm_b_im[0])
    bm, cm, a_re8, a_im8 = _ssm_matrices(a_re, a_im, bb_re, bb_im, ssm_c_re[0], ssm_c_im[0])
    y = _ssm(u_c, u_l, bm, cm, a_re8, a_im8)

    mods = [lat(m) for m in (sh1, sc1, g1, sh2, sc2, g2)]
    return _merge(x, y, u_l, attn, mods, n1, norm2_g[0][None, :],
                  w_gate, ssm_d[0][None, :], w_glu[0].astype(BF16), b_glu[0][None, :],
                  w_br_attn[0].astype(BF16), w_br_ssm[0].astype(BF16), w_out[0].astype(BF16),
                  w_mlp1[0].astype(BF16), b_mlp1[0][None, :], w_mlp2[0].astype(BF16), b_mlp2[0][None, :])
```

```python
import functools

import jax
import jax.numpy as jnp
from jax import lax
from jax.experimental import pallas as pl
from jax.experimental.pallas import tpu as pltpu

F32 = jnp.float32
BF16 = jnp.bfloat16

D_MODEL = 1024
N_HEADS = 16
N_KV_HEADS = 4
HEAD_DIM = 64
KV_REP = N_HEADS // N_KV_HEADS
GRID_W = 64
ROPE_THETA = 10000.0
ROPE_AXIS_DIM = HEAD_DIM // 2
D_SSM = 512
SSM_GROUP = 16
N_SSM_GROUPS = 32
SSM_STATE = 64
N_STATE = N_SSM_GROUPS * SSM_STATE
D_FF = 4 * D_MODEL
NORM_EPS = 1e-6
OFF_K = 1024
OFF_V = 1280
OFF_U = 1536
OFF_GA = 2048

LANES = 128
SUBLANES = 8
MXU_TILE = 256
LOG2_E = 1.4426950408889634
VMEM_LIMIT_BYTES = 56 * 1024 * 1024

PROJ_ROWS = 512
PROJ_SUB_ROWS = 256
ATTN_ROWS = 256
ATTN_KEYS = 256
ONES_ROWS = 16
MERGE_ROWS = 512
SSM_STEPS = 128
SSM_BATCH = SUBLANES
SCAN_LANES = 1024
SCAN_UNROLL = 8
FF_CHUNK = 1024


def _const_spec(shape):
    nd = len(shape)
    return pl.BlockSpec(shape, lambda *_: (0,) * nd, pipeline_mode=pl.Buffered(1))


def _params(n_axes, vmem=VMEM_LIMIT_BYTES, flags=None):
    return pltpu.CompilerParams(dimension_semantics=("arbitrary",) * n_axes, vmem_limit_bytes=vmem,
                                flags=flags)


def _mod_body(c_ref, w_ref, b_ref, o_ref):
    c = c_ref[...]
    s = c * jax.nn.sigmoid(c)
    o_ref[...] = jnp.dot(s, w_ref[...], preferred_element_type=F32) + b_ref[...]


def _modulation(c_rows, w_mod, b_mod):
    rows = c_rows.shape[0]
    ncol = w_mod.shape[1]
    blk = D_MODEL
    return pl.pallas_call(
        _mod_body,
        grid=(ncol // blk,),
        in_specs=[pl.BlockSpec((rows, D_MODEL), lambda j: (0, 0)),
                  pl.BlockSpec((D_MODEL, blk), lambda j: (0, j)),
                  pl.BlockSpec((1, blk), lambda j: (0, j))],
        out_specs=pl.BlockSpec((rows, blk), lambda j: (0, j)),
        out_shape=jax.ShapeDtypeStruct((rows, ncol), F32),
        compiler_params=_params(1),
        name="mod",
    )(c_rows, w_mod, b_mod)


def _disc_body(lr_ref, li_ref, ldt_ref, br_ref, bi_ref, ar_ref, ai_ref, bbr_ref, bbi_ref):
    lr = lr_ref[...]
    li = li_ref[...]
    dt = jnp.exp(ldt_ref[...])
    mag = jnp.exp(lr * dt)
    ar = mag * jnp.cos(li * dt)
    ai = mag * jnp.sin(li * dt)
    nr = ar - 1.0
    den = lr * lr + li * li
    cr = (nr * lr + ai * li) / den
    ci = (ai * lr - nr * li) / den
    br = br_ref[...]
    bi = bi_ref[...]
    ar_ref[...] = ar
    ai_ref[...] = ai
    bbr_ref[...] = cr * br - ci * bi
    bbi_ref[...] = cr * bi + ci * br


def _discretize(lam_re, lam_im, log_dt, b_re, b_im):
    rows = 2 * N_SSM_GROUPS
    cols = SSM_STATE * SSM_GROUP
    rep = lambda a: jnp.repeat(a.reshape(rows, SSM_STATE), SSM_GROUP, axis=1)
    ldt = jnp.broadcast_to(log_dt.reshape(rows, 1), (rows, cols))
    spec = pl.BlockSpec((rows, cols), lambda: (0, 0))
    sds = jax.ShapeDtypeStruct((rows, cols), F32)
    ar, ai, bbr, bbi = pl.pallas_call(
        _disc_body,
        in_specs=[spec] * 5,
        out_specs=[spec] * 4,
        out_shape=[sds] * 4,
        name="disc",
    )(rep(lam_re), rep(lam_im), ldt, b_re.reshape(rows, cols), b_im.reshape(rows, cols))
    a_re = ar[:, ::SSM_GROUP].reshape(2, N_STATE)
    a_im = ai[:, ::SSM_GROUP].reshape(2, N_STATE)
    shape = (2, N_SSM_GROUPS, SSM_STATE, SSM_GROUP)
    return a_re, a_im, bbr.reshape(shape), bbi.reshape(shape)


def _ssm_matrices(a_re, a_im, bb_re, bb_im, c_re, c_im):
    eye = jnp.eye(16, dtype=F32)

    def in_blocks(bb):
        t = bb.reshape(2, 2, 16, SSM_STATE, SSM_GROUP).transpose(0, 1, 2, 4, 3)
        return jnp.einsum('dkghp,gG->dkghGp', t, eye).reshape(2, 2, 256, 1024)

    def out_blocks(c):
        t = c.reshape(2, 2, 16, SSM_GROUP, SSM_STATE).transpose(0, 1, 2, 4, 3)
        return jnp.einsum('dkgph,gG->dkgpGh', t, eye).reshape(2, 2, 1024, 256)

    bm = jnp.concatenate([in_blocks(bb_re), in_blocks(bb_im)], axis=-1).astype(BF16)
    cm = jnp.stack([out_blocks(c_re), out_blocks(-c_im)], axis=2).astype(BF16)
    bcast = lambda a: jnp.broadcast_to(a[:, None, :], (2, SUBLANES, N_STATE))
    return bm, cm, bcast(a_re), bcast(a_im)


def _rms_modulate(x, g, sc, sh):
    ms = jnp.mean(x * x, axis=-1, keepdims=True)
    xn = x * lax.rsqrt(ms + NORM_EPS) * g
    return xn * (1.0 + sc) + sh


def _proj_body(*refs, n_norm, rope):
    if rope:
        (x_ref, sh_ref, sc_ref, g_ref, wm_ref, wvt_ref, mavg_ref, gn_ref, cos_ref, sin_ref,
         q_ref, k_ref, vt_ref, u_ref) = refs
    else:
        (x_ref, sh_ref, sc_ref, g_ref, wm_ref, wvt_ref, mavg_ref, gn_ref,
         k_ref, vt_ref, u_ref) = refs
    rows = x_ref.shape[0]
    sub = min(rows, PROJ_SUB_ROWS)
    mavg = mavg_ref[...]
    width = mavg.shape[0]
    n_q_chunks = n_norm // LANES - k_ref.shape[0]
    if rope:
        lane = lax.broadcasted_iota(jnp.int32, (sub, LANES), 1)
        half = ROPE_AXIS_DIM // 2
        first_half = (lane % ROPE_AXIS_DIM) < half
    for r0 in range(0, rows, sub):
        rs = slice(r0, r0 + sub)
        h = _rms_modulate(x_ref[rs, :], g_ref[...], sc_ref[...], sh_ref[...])
        hb = h.astype(BF16)
        p = jnp.dot(hb, wm_ref[...], preferred_element_type=F32)
        qk = p[:, :n_norm]
        u_ref[rs, :] = p[:, n_norm:]
        sq = (qk * qk).astype(BF16)
        msq = jnp.concatenate(
            [jnp.dot(sq[:, width * j:width * (j + 1)], mavg, preferred_element_type=F32)
             for j in range(n_norm // width)], axis=1)
        vt = lax.dot_general(wvt_ref[...], hb, (((1,), (1,)), ((), ())), preferred_element_type=F32)
        for g in range(N_KV_HEADS):
            vt_ref[g, :, rs] = vt[HEAD_DIM * g:HEAD_DIM * (g + 1), :].astype(BF16)
        qkn = qk * lax.rsqrt(msq + NORM_EPS) * gn_ref[...]
        for c in range(n_norm // LANES):
            xc = qkn[:, LANES * c:LANES * (c + 1)]
            if rope:
                partner = jnp.where(first_half, pltpu.roll(xc, LANES - half, 1), pltpu.roll(xc, half, 1))
                xc = xc * cos_ref[rs, :] + partner * sin_ref[rs, :]
            if c < n_q_chunks:
                q_ref[c, rs, :] = xc.astype(BF16)
            else:
                k_ref[c - n_q_chunks, rs, :] = xc.astype(BF16)


def _project(x, sh, sc, g, wm, wvt, mavg, gn, cos, sin):
    bsz, n, _ = x.shape
    tm = min(PROJ_ROWS, n)
    rope = cos is not None
    n_norm = gn.shape[1]
    ncol = wm.shape[1]
    kv_w = N_KV_HEADS * HEAD_DIM
    q_chunks = D_MODEL // LANES
    mod_rows = sh.shape[0]
    mod_spec = pl.BlockSpec((None, 1, D_MODEL),
                            (lambda i, b: (b, 0, 0)) if mod_rows == bsz else (lambda i, b: (0, 0, 0)))
    in_specs = [pl.BlockSpec((None, tm, D_MODEL), lambda i, b: (b, i, 0)),
                mod_spec, mod_spec,
                _const_spec((1, D_MODEL)),
                _const_spec((D_MODEL, ncol)),
                _const_spec((kv_w, D_MODEL)),
                _const_spec(mavg.shape),
                _const_spec((1, n_norm))]
    args = [x, sh, sc, g, wm, wvt, mavg, gn]
    out_specs = [pl.BlockSpec((None, N_KV_HEADS, tm, LANES), lambda i, b: (b, 0, i, 0)),
                 pl.BlockSpec((None, N_KV_HEADS, HEAD_DIM, tm), lambda i, b: (b, 0, 0, i)),
                 pl.BlockSpec((None, tm, D_SSM), lambda i, b: (b, i, 0))]
    out_shape = [jax.ShapeDtypeStruct((bsz, N_KV_HEADS, n, LANES), BF16),
                 jax.ShapeDtypeStruct((bsz, N_KV_HEADS, HEAD_DIM, n), BF16),
                 jax.ShapeDtypeStruct((bsz, n, D_SSM), F32)]
    if rope:
        in_specs += [pl.BlockSpec((tm, LANES), lambda i, b: (i, 0))] * 2
        args += [cos, sin]
        out_specs = [pl.BlockSpec((None, q_chunks, tm, LANES), lambda i, b: (b, 0, i, 0))] + out_specs
        out_shape = [jax.ShapeDtypeStruct((bsz, q_chunks, n, LANES), BF16)] + out_shape
    outs = pl.pallas_call(
        functools.partial(_proj_body, n_norm=n_norm, rope=rope),
        grid=(n // tm, bsz),
        in_specs=in_specs,
        out_specs=out_specs,
        out_shape=out_shape,
        compiler_params=_params(2),
        name="proj_latent" if rope else "proj_ctx",
    )(*args)
    return outs if rope else [None] + list(outs)


def _attn_body(q_ref, qn_ref, kl_ref, kc_ref, vtl_ref, vtc_ref, o_ref, s0_scr, s1_scr, m_scr, o_scr):
    n_ctx = kc_ref.shape[1]
    n_lat = kl_ref.shape[1]
    n_blocks = (n_ctx + n_lat) // ATTN_KEYS
    tq = q_ref.shape[1]
    lane = lax.broadcasted_iota(jnp.int32, (tq, LANES), 1)
    head_lanes = (lane < HEAD_DIM, lane >= HEAD_DIM)
    s_bufs = (s0_scr, s1_scr)
    nt = (((1,), (1,)), ((), ()))
    reload_base = jnp.minimum(pl.program_id(1), 0)

    def slab_max(s):
        return jnp.max(s.reshape(s.shape[0] // SUBLANES, SUBLANES, tq), axis=0)

    def scores(qc, g, r, slot):
        s_scr = s_bufs[slot]
        qm = jnp.where(head_lanes[r % 2], qc, jnp.zeros_like(qc))
        sc = lax.dot_general(kc_ref[g], qm, nt, preferred_element_type=F32)
        sl = lax.dot_general(kl_ref[g], qm, nt, preferred_element_type=F32)
        s_scr[:n_ctx, :] = sc
        s_scr[n_ctx:, :] = sl
        m_scr[slot] = jnp.maximum(slab_max(sc), slab_max(sl))

    ones_rows = jnp.ones((ONES_ROWS, ATTN_KEYS), BF16)

    def finish(g, r, slot):
        s_scr = s_bufs[slot]
        m = jnp.max(m_scr[slot], axis=0, keepdims=True)
        m8 = jnp.broadcast_to(m, (SUBLANES, tq))
        acc = jnp.zeros((HEAD_DIM + ONES_ROWS, tq), F32)
        for kb in range(n_blocks):
            lo = kb * ATTN_KEYS
            row = pl.multiple_of(reload_base + lo, ATTN_KEYS)
            s = s_scr[pl.ds(row, ATTN_KEYS), :].reshape(ATTN_KEYS // SUBLANES, SUBLANES, tq)
            pb = jnp.exp2(s - m8).reshape(ATTN_KEYS, tq).astype(BF16)
            if lo < n_ctx:
                vt = vtc_ref[g, :, lo:lo + ATTN_KEYS]
            else:
                vt = vtl_ref[g, :, lo - n_ctx:lo - n_ctx + ATTN_KEYS]
            acc = acc + jnp.dot(jnp.concatenate([vt, ones_rows], axis=0), pb, preferred_element_type=F32)
        o_scr[KV_REP * g + r] = acc

    @pl.when(pl.program_id(1) == 0)
    def _():
        scores(q_ref[0], 0, 0, 0)

    def group(g, carry):
        for r in range(KV_REP):
            if r + 1 < KV_REP:
                scores(q_ref[2 * g + (r + 1) // 2], g, r + 1, (r + 1) % 2)
            else:
                last = g == N_KV_HEADS - 1
                qc = jnp.where(last, qn_ref[0], q_ref[2 * jnp.minimum(g + 1, N_KV_HEADS - 1)])
                scores(qc, jnp.where(last, 0, g + 1), 0, (r + 1) % 2)
            finish(g, r, r % 2)
        return carry

    lax.fori_loop(0, N_KV_HEADS, group, 0)
    o_all = o_scr[...]
    o_all = o_all[:, :HEAD_DIM, :] / o_all[:, HEAD_DIM:HEAD_DIM + 1, :]
    o_ref[...] = o_all.reshape(N_HEADS * HEAD_DIM, tq).T.astype(BF16)


def _attention(q, k_l, k_c, vt_l, vt_c):
    bsz, q_chunks, n, _ = q.shape
    n_ctx = k_c.shape[2]
    tq = ATTN_ROWS
    whole = lambda shape: pl.BlockSpec((None,) + shape, lambda b, j: (b,) + (0,) * len(shape))
    return pl.pallas_call(
        _attn_body,
        grid=(bsz, n // tq),
        in_specs=[pl.BlockSpec((None, q_chunks, tq, LANES), lambda b, j: (b, 0, j, 0)),
                  pl.BlockSpec((None, 1, tq, LANES), lambda b, j: (b, 0, jnp.minimum(j + 1, n // tq - 1), 0)),
                  whole((N_KV_HEADS, n, LANES)),
                  whole((N_KV_HEADS, n_ctx, LANES)),
                  whole((N_KV_HEADS, HEAD_DIM, n)),
                  whole((N_KV_HEADS, HEAD_DIM, n_ctx))],
        out_specs=pl.BlockSpec((None, tq, D_MODEL), lambda b, j: (b, j, 0)),
        out_shape=jax.ShapeDtypeStruct((bsz, n, D_MODEL), BF16),
        scratch_shapes=[pltpu.VMEM((n_ctx + n, tq), F32),
                        pltpu.VMEM((n_ctx + n, tq), F32),
                        pltpu.VMEM((2, SUBLANES, tq), F32),
                        pltpu.VMEM((N_HEADS, HEAD_DIM + ONES_ROWS, tq), F32)],
        compiler_params=_params(2),
        name="attn",
    )(q, q, k_l, k_c, vt_l, vt_c)


def _ssm_body(uc_ref, ul_ref, bm_ref, cm_ref, ar_ref, ai_ref, y_ref,
              ut_scr, yt_scr, st_scr, hr_scr, hi_scr, *, n_ctx_tiles):
    d = pl.program_id(0)
    i = pl.program_id(2)
    steps = uc_ref.shape[1]
    half_w = N_STATE // 2
    n_slabs = D_SSM // LANES

    @pl.when(i == 0)
    def _():
        hr_scr[...] = jnp.zeros_like(hr_scr)
        hi_scr[...] = jnp.zeros_like(hi_scr)

    u = jnp.where(i < n_ctx_tiles, uc_ref[...], ul_ref[...])
    for b in range(SSM_BATCH):
        for c in range(n_slabs):
            ut_scr[c, pl.ds(b, steps, stride=SSM_BATCH), :] = u[b, :, LANES * c:LANES * (c + 1)]
    ub = jnp.concatenate([ut_scr[c] for c in range(n_slabs)], axis=1).astype(BF16)
    for k in range(2):
        r = jnp.dot(ub[:, 256 * k:256 * (k + 1)], bm_ref[k], preferred_element_type=F32)
        st_scr[:, half_w * k:half_w * (k + 1)] = r[:, :half_w]
        st_scr[:, N_STATE + half_w * k:N_STATE + half_w * (k + 1)] = r[:, half_w:]

    def scan(forward):
        for c in range(N_STATE // SCAN_LANES):
            re = slice(SCAN_LANES * c, SCAN_LANES * (c + 1))
            im = slice(N_STATE + SCAN_LANES * c, N_STATE + SCAN_LANES * (c + 1))
            a_r = ar_ref[:, re]
            a_i = ai_ref[:, re]

            def step(j, carry):
                hr, hi = carry
                t = j if forward else steps - 1 - j
                row = pl.multiple_of(t * SSM_BATCH, SSM_BATCH)
                br = st_scr[pl.ds(row, SSM_BATCH), re]
                bi = st_scr[pl.ds(row, SSM_BATCH), im]
                hr, hi = a_r * hr - a_i * hi + br, a_r * hi + a_i * hr + bi
                st_scr[pl.ds(row, SSM_BATCH), re] = hr
                st_scr[pl.ds(row, SSM_BATCH), im] = hi
                return hr, hi

            hr, hi = lax.fori_loop(0, steps, step, (hr_scr[:, re], hi_scr[:, re]), unroll=SCAN_UNROLL)
            hr_scr[:, re] = hr
            hi_scr[:, re] = hi

    @pl.when(d == 0)
    def _():
        scan(True)

    @pl.when(d == 1)
    def _():
        scan(False)

    @pl.when(i >= n_ctx_tiles)
    def _():
        for k in range(2):
            y = (jnp.dot(st_scr[:, half_w * k:half_w * (k + 1)].astype(BF16), cm_ref[k, 0],
                         preferred_element_type=F32)
                 + jnp.dot(st_scr[:, N_STATE + half_w * k:N_STATE + half_w * (k + 1)].astype(BF16), cm_ref[k, 1],
                           preferred_element_type=F32))
            for c in range(2):
                yt_scr[2 * k + c] = y[:, LANES * c:LANES * (c + 1)]
        for b in range(SSM_BATCH):
            y_ref[b] = jnp.concatenate(
                [yt_scr[c, pl.ds(b, steps, stride=SSM_BATCH), :] for c in range(n_slabs)], axis=1)


def _ssm(u_c, u_l, bm, cm, a_re, a_im):
    bsz, n_ctx, _ = u_c.shape
    n = u_l.shape[1]
    t = SSM_STEPS
    nct = n_ctx // t
    nlt = n // t
    rows = t * SSM_BATCH

    def ctx_tile(d, i):
        return jnp.where(d == 0, jnp.minimum(i, nct - 1), jnp.maximum(nct - 1 - i, 0))

    def lat_tile(d, i):
        j = jnp.maximum(i - nct, 0)
        return jnp.where(d == 0, j, nlt - 1 - j)

    once = pl.Buffered(1)
    return pl.pallas_call(
        functools.partial(_ssm_body, n_ctx_tiles=nct),
        grid=(2, bsz // SSM_BATCH, nct + nlt),
        in_specs=[pl.BlockSpec((SSM_BATCH, t, D_SSM), lambda d, bh, i: (bh, ctx_tile(d, i), 0)),
                  pl.BlockSpec((SSM_BATCH, t, D_SSM), lambda d, bh, i: (bh, lat_tile(d, i), 0)),
                  pl.BlockSpec((None, 2, 256, N_STATE), lambda d, bh, i: (d, 0, 0, 0), pipeline_mode=once),
                  pl.BlockSpec((None, 2, 2, N_STATE // 2, 256), lambda d, bh, i: (d, 0, 0, 0, 0),
                               pipeline_mode=once),
                  pl.BlockSpec((None, SUBLANES, N_STATE), lambda d, bh, i: (d, 0, 0), pipeline_mode=once),
                  pl.BlockSpec((None, SUBLANES, N_STATE), lambda d, bh, i: (d, 0, 0), pipeline_mode=once)],
        out_specs=pl.BlockSpec((None, SSM_BATCH, t, D_SSM), lambda d, bh, i: (d, bh, lat_tile(d, i), 0)),
        out_shape=jax.ShapeDtypeStruct((2, bsz, n, D_SSM), F32),
        scratch_shapes=[pltpu.VMEM((D_SSM // LANES, rows, LANES), F32),
                        pltpu.VMEM((D_SSM // LANES, rows, LANES), F32),
                        pltpu.VMEM((rows, 2 * N_STATE), F32),
                        pltpu.VMEM((SSM_BATCH, N_STATE), F32),
                        pltpu.VMEM((SSM_BATCH, N_STATE), F32)],
        compiler_params=_params(3),
        name="ssm",
    )(u_c, u_l, bm, cm, a_re, a_im)


def _merge_body(x_ref, yf_ref, yb_ref, u_ref, at_ref,
                sh1_ref, sc1_ref, g1_ref, sh2_ref, sc2_ref, g2_ref, n1_ref, n2_ref,
                wg_ref, dv_ref, wglu_ref, bglu_ref, wba_ref, wbs_ref, wo_ref,
                w1_ref, b1_ref, w2_ref, b2_ref, o_ref):
    x = x_ref[...]
    hb = _rms_modulate(x, n1_ref[...], sc1_ref[...], sh1_ref[...]).astype(BF16)
    gates = jnp.dot(hb, wg_ref[...], preferred_element_type=F32)
    gate_a = jax.nn.sigmoid(gates[:, :D_MODEL])
    gate_s = jax.nn.sigmoid(gates[:, D_MODEL:])

    y = yf_ref[...] + yb_ref[...] + dv_ref[...] * u_ref[...]
    y = jax.nn.gelu(y)
    z = jnp.dot(y.astype(BF16), wglu_ref[...], preferred_element_type=F32) + bglu_ref[...]
    y = y * jax.nn.sigmoid(z)

    merged = (gate_a * jnp.dot(at_ref[...], wba_ref[...], preferred_element_type=F32)
              + gate_s * jnp.dot(y.astype(BF16), wbs_ref[...], preferred_element_type=F32))
    mix = jnp.dot(merged.astype(BF16), wo_ref[...], preferred_element_type=F32)
    x1 = x + g1_ref[...] * mix

    h2 = _rms_modulate(x1, n2_ref[...], sc2_ref[...], sh2_ref[...]).astype(BF16)
    acc = jnp.zeros_like(x1)
    for c in range(D_FF // FF_CHUNK):
        cols = slice(FF_CHUNK * c, FF_CHUNK * (c + 1))
        hid = jnp.dot(h2, w1_ref[:, cols], preferred_element_type=F32) + b1_ref[:, cols]
        hid = jnp.square(jnp.maximum(hid, 0.0))
        acc = acc + jnp.dot(hid.astype(BF16), w2_ref[cols, :], preferred_element_type=F32)
    o_ref[...] = x1 + g2_ref[...] * (acc + b2_ref[...])


def _merge(x, y2, u2, attn, mods, n1, n2, wg, dvec, wglu, bglu, wba, wbs, wo, w1, b1, w2, b2):
    bsz, n, _ = x.shape
    tm = MERGE_ROWS
    row_spec = pl.BlockSpec((None, tm, D_MODEL), lambda b, i: (b, i, 0))
    mod_spec = pl.BlockSpec((None, 1, D_MODEL), lambda b, i: (b, 0, 0))
    in_specs = [row_spec,
                pl.BlockSpec((None, None, tm, D_SSM), lambda b, i: (0, b, i, 0)),
                pl.BlockSpec((None, None, tm, D_SSM), lambda b, i: (1, b, i, 0)),
                pl.BlockSpec((None, tm, D_SSM), lambda b, i: (b, i, 0)),
                row_spec] + [mod_spec] * 6
    consts = [n1, n2, wg, dvec, wglu, bglu, wba, wbs, wo, w1, b1, w2, b2]
    in_specs += [_const_spec(a.shape) for a in consts]
    return pl.pallas_call(
        _merge_body,
        grid=(bsz, n // tm),
        in_specs=in_specs,
        out_specs=row_spec,
        out_shape=jax.ShapeDtypeStruct(x.shape, x.dtype),
        compiler_params=_params(2),
        name="merge",
    )(x, y2, y2, u2, attn, *mods, *consts)


def _rope_tables(n_tok):
    half = ROPE_AXIS_DIM // 2
    pos = jnp.arange(n_tok, dtype=jnp.int32)
    row = (pos // GRID_W).astype(F32)
    col = (pos % GRID_W).astype(F32)
    inv_freq = ROPE_THETA ** (-jnp.arange(half, dtype=F32) / half)
    ang_r = row[:, None] * inv_freq
    ang_c = col[:, None] * inv_freq
    cos = jnp.concatenate([jnp.cos(ang_r)] * 2 + [jnp.cos(ang_c)] * 2, axis=-1)
    sin = jnp.concatenate([-jnp.sin(ang_r), jnp.sin(ang_r), -jnp.sin(ang_c), jnp.sin(ang_c)], axis=-1)
    reps = LANES // HEAD_DIM
    return jnp.tile(cos, (1, reps)), jnp.tile(sin, (1, reps))


def kernel(x, c, ctx, c_ctx, w_mod, b_mod, norm1_g, norm2_g, w_in, q_norm_g, k_norm_g,
           ssm_lambda_re, ssm_lambda_im, ssm_log_dt, ssm_b_re, ssm_b_im, ssm_c_re, ssm_c_im,
           ssm_d, w_glu, b_glu, w_br_attn, w_br_ssm, w_out, w_mlp1, b_mlp1, w_mlp2, b_mlp2):
    bsz, n_tok, d_model = x.shape
    n_ctx = ctx.shape[1]
    assert d_model == D_MODEL and w_mod.shape[0] == 1, "single-layer block of width 1024 only"
    assert n_tok % PROJ_ROWS == 0 and n_ctx % min(PROJ_ROWS, n_ctx) == 0 and n_ctx % SSM_STEPS == 0
    assert bsz % SSM_BATCH == 0 and n_tok % GRID_W == 0

    pad = (-(bsz + 1)) % SUBLANES
    c_rows = jnp.concatenate([c, c_ctx[None, :], jnp.zeros((pad, D_MODEL), F32)], axis=0)
    mod = _modulation(c_rows, w_mod[0], b_mod[0][None, :])
    sh1, sc1, g1, sh2, sc2, g2 = [m[:, None, :] for m in jnp.split(mod, 6, axis=-1)]
    lat = lambda m: m[:bsz]
    cx = lambda m: m[bsz:bsz + 1]

    w = w_in[0]
    w_k = w[:, OFF_K:OFF_V].reshape(D_MODEL, N_KV_HEADS, 1, HEAD_DIM)
    w_kdup = jnp.broadcast_to(w_k, (D_MODEL, N_KV_HEADS, 2, HEAD_DIM)).reshape(D_MODEL, 2 * OFF_V - 2 * OFF_K)
    w_u = w[:, OFF_U:OFF_GA]
    wm_lat = jnp.concatenate([w[:, :OFF_K], w_kdup, w_u], axis=1).astype(BF16)
    wm_ctx = jnp.concatenate([w_kdup, w_u], axis=1).astype(BF16)
    w_vt = w[:, OFF_V:OFF_U].T.astype(BF16)
    w_gate = w[:, OFF_GA:].astype(BF16)
    mavg = jnp.kron(jnp.eye(MXU_TILE // HEAD_DIM, dtype=F32),
                    jnp.full((HEAD_DIM, HEAD_DIM), 1.0 / HEAD_DIM, F32)).astype(BF16)
    gk = jnp.tile(k_norm_g[0], 2 * N_KV_HEADS)
    gq = jnp.tile(q_norm_g[0], N_HEADS) * (HEAD_DIM ** -0.5 * LOG2_E)
    gn_lat = jnp.concatenate([gq, gk])[None, :]
    gn_ctx = gk[None, :]
    cos, sin = _rope_tables(n_tok)
    n1 = norm1_g[0][None, :]

    q, k_l, vt_l, u_l = _project(x, lat(sh1), lat(sc1), n1, wm_lat, w_vt, mavg, gn_lat, cos, sin)
    _, k_c, vt_c, u_c = _project(ctx, cx(sh1), cx(sc1), n1, wm_ctx, w_vt, mavg, gn_ctx, None, None)

    attn = _attention(q, k_l, k_c, vt_l, vt_c)

    a_re, a_im, bb_re, bb_im = _discretize(ssm_lambda_re[0], ssm_lambda_im[0], ssm_log_dt[0],
                                           ssm_b_re[0], ssm_b_im[0])
    bm, cm, a_re8, a_im8 = _ssm_matrices(a_re, a_im, bb_re, bb_im, ssm_c_re[0], ssm_c_im[0])
    y = _ssm(u_c, u_l, bm, cm, a_re8, a_im8)

    mods = [lat(m) for m in (sh1, sc1, g1, sh2, sc2, g2)]
    return _merge(x, y, u_l, attn, mods, n1, norm2_g[0][None, :],
                  w_gate, ssm_d[0][None, :], w_glu[0].astype(BF16), b_glu[0][None, :],
                  w_br_attn[0].astype(BF16), w_br_ssm[0].astype(BF16), w_out[0].astype(BF16),
                  w_mlp1[0].astype(BF16), b_mlp1[0][None, :], w_mlp2[0].astype(BF16), b_mlp2[0][None, :])
```

```python
import functools

import jax
import jax.numpy as jnp
from jax import lax
from jax.experimental import pallas as pl
from jax.experimental.pallas import tpu as pltpu

F32 = jnp.float32
BF16 = jnp.bfloat16

D_MODEL = 1024
N_HEADS = 16
N_KV_HEADS = 4
HEAD_DIM = 64
KV_REP = N_HEADS // N_KV_HEADS
GRID_W = 64
ROPE_THETA = 10000.0
ROPE_AXIS_DIM = HEAD_DIM // 2
D_SSM = 512
SSM_GROUP = 16
N_SSM_GROUPS = 32
SSM_STATE = 64
N_STATE = N_SSM_GROUPS * SSM_STATE
D_FF = 4 * D_MODEL
NORM_EPS = 1e-6
OFF_K = 1024
OFF_V = 1280
OFF_U = 1536
OFF_GA = 2048

LANES = 128
SUBLANES = 8
MXU_TILE = 256
LOG2_E = 1.4426950408889634
VMEM_LIMIT_BYTES = 56 * 1024 * 1024

PROJ_ROWS = 1024
PROJ_SUB_ROWS = 256
ATTN_ROWS = 256
ATTN_KEYS = 256
ONES_ROWS = 16
MERGE_ROWS = 512
SSM_STEPS = 128
SSM_BATCH = SUBLANES
SCAN_LANES = 1024
SCAN_UNROLL = 8
FF_CHUNK = 1024


def _const_spec(shape):
    nd = len(shape)
    return pl.BlockSpec(shape, lambda *_: (0,) * nd, pipeline_mode=pl.Buffered(1))


def _params(n_axes, vmem=VMEM_LIMIT_BYTES, flags=None):
    return pltpu.CompilerParams(dimension_semantics=("arbitrary",) * n_axes, vmem_limit_bytes=vmem,
                                flags=flags)


def _mod_body(c_ref, w_ref, b_ref, o_ref):
    c = c_ref[...]
    s = c * jax.nn.sigmoid(c)
    o_ref[...] = jnp.dot(s, w_ref[...], preferred_element_type=F32) + b_ref[...]


def _modulation(c_rows, w_mod, b_mod):
    rows = c_rows.shape[0]
    ncol = w_mod.shape[1]
    blk = D_MODEL
    return pl.pallas_call(
        _mod_body,
        grid=(ncol // blk,),
        in_specs=[pl.BlockSpec((rows, D_MODEL), lambda j: (0, 0)),
                  pl.BlockSpec((D_MODEL, blk), lambda j: (0, j)),
                  pl.BlockSpec((1, blk), lambda j: (0, j))],
        out_specs=pl.BlockSpec((rows, blk), lambda j: (0, j)),
        out_shape=jax.ShapeDtypeStruct((rows, ncol), F32),
        compiler_params=_params(1),
        name="mod",
    )(c_rows, w_mod, b_mod)


def _disc_body(lr_ref, li_ref, ldt_ref, br_ref, bi_ref, ar_ref, ai_ref, bbr_ref, bbi_ref):
    lr = lr_ref[...]
    li = li_ref[...]
    dt = jnp.exp(ldt_ref[...])
    mag = jnp.exp(lr * dt)
    ar = mag * jnp.cos(li * dt)
    ai = mag * jnp.sin(li * dt)
    nr = ar - 1.0
    den = lr * lr + li * li
    cr = (nr * lr + ai * li) / den
    ci = (ai * lr - nr * li) / den
    br = br_ref[...]
    bi = bi_ref[...]
    ar_ref[...] = ar
    ai_ref[...] = ai
    bbr_ref[...] = cr * br - ci * bi
    bbi_ref[...] = cr * bi + ci * br


def _discretize(lam_re, lam_im, log_dt, b_re, b_im):
    rows = 2 * N_SSM_GROUPS
    cols = SSM_STATE * SSM_GROUP
    rep = lambda a: jnp.repeat(a.reshape(rows, SSM_STATE), SSM_GROUP, axis=1)
    ldt = jnp.broadcast_to(log_dt.reshape(rows, 1), (rows, cols))
    spec = pl.BlockSpec((rows, cols), lambda: (0, 0))
    sds = jax.ShapeDtypeStruct((rows, cols), F32)
    ar, ai, bbr, bbi = pl.pallas_call(
        _disc_body,
        in_specs=[spec] * 5,
        out_specs=[spec] * 4,
        out_shape=[sds] * 4,
        name="disc",
    )(rep(lam_re), rep(lam_im), ldt, b_re.reshape(rows, cols), b_im.reshape(rows, cols))
    a_re = ar[:, ::SSM_GROUP].reshape(2, N_STATE)
    a_im = ai[:, ::SSM_GROUP].reshape(2, N_STATE)
    shape = (2, N_SSM_GROUPS, SSM_STATE, SSM_GROUP)
    return a_re, a_im, bbr.reshape(shape), bbi.reshape(shape)


def _ssm_matrices(a_re, a_im, bb_re, bb_im, c_re, c_im):
    eye = jnp.eye(16, dtype=F32)

    def in_blocks(bb):
        t = bb.reshape(2, 2, 16, SSM_STATE, SSM_GROUP).transpose(0, 1, 2, 4, 3)
        return jnp.einsum('dkghp,gG->dkghGp', t, eye).reshape(2, 2, 256, 1024)

    def out_blocks(c):
        t = c.reshape(2, 2, 16, SSM_GROUP, SSM_STATE).transpose(0, 1, 2, 4, 3)
        return jnp.einsum('dkgph,gG->dkgpGh', t, eye).reshape(2, 2, 1024, 256)

    bm = jnp.concatenate([in_blocks(bb_re), in_blocks(bb_im)], axis=-1).astype(BF16)
    cm = jnp.stack([out_blocks(c_re), out_blocks(-c_im)], axis=2).astype(BF16)
    bcast = lambda a: jnp.broadcast_to(a[:, None, :], (2, SUBLANES, N_STATE))
    return bm, cm, bcast(a_re), bcast(a_im)


def _rms_modulate(x, g, sc, sh):
    ms = jnp.mean(x * x, axis=-1, keepdims=True)
    xn = x * lax.rsqrt(ms + NORM_EPS) * g
    return xn * (1.0 + sc) + sh


def _proj_body(*refs, n_norm, rope):
    if rope:
        (x_ref, sh_ref, sc_ref, g_ref, wm_ref, wvt_ref, mavg_ref, gn_ref, cos_ref, sin_ref,
         q_ref, k_ref, vt_ref, u_ref) = refs
    else:
        (x_ref, sh_ref, sc_ref, g_ref, wm_ref, wvt_ref, mavg_ref, gn_ref,
         k_ref, vt_ref, u_ref) = refs
    rows = x_ref.shape[0]
    sub = min(rows, PROJ_SUB_ROWS)
    mavg = mavg_ref[...]
    width = mavg.shape[0]
    n_q_chunks = n_norm // LANES - k_ref.shape[0]
    if rope:
        lane = lax.broadcasted_iota(jnp.int32, (sub, LANES), 1)
        half = ROPE_AXIS_DIM // 2
        first_half = (lane % ROPE_AXIS_DIM) < half
    for r0 in range(0, rows, sub):
        rs = slice(r0, r0 + sub)
        h = _rms_modulate(x_ref[rs, :], g_ref[...], sc_ref[...], sh_ref[...])
        hb = h.astype(BF16)
        p = jnp.dot(hb, wm_ref[...], preferred_element_type=F32)
        qk = p[:, :n_norm]
        u_ref[rs, :] = p[:, n_norm:]
        sq = (qk * qk).astype(BF16)
        msq = jnp.concatenate(
            [jnp.dot(sq[:, width * j:width * (j + 1)], mavg, preferred_element_type=F32)
             for j in range(n_norm // width)], axis=1)
        vt = lax.dot_general(wvt_ref[...], hb, (((1,), (1,)), ((), ())), preferred_element_type=F32)
        for g in range(N_KV_HEADS):
            vt_ref[g, :, rs] = vt[HEAD_DIM * g:HEAD_DIM * (g + 1), :].astype(BF16)
        qkn = qk * lax.rsqrt(msq + NORM_EPS) * gn_ref[...]
        for c in range(n_norm // LANES):
            xc = qkn[:, LANES * c:LANES * (c + 1)]
            if rope:
                partner = jnp.where(first_half, pltpu.roll(xc, LANES - half, 1), pltpu.roll(xc, half, 1))
                xc = xc * cos_ref[rs, :] + partner * sin_ref[rs, :]
            if c < n_q_chunks:
                q_ref[c, rs, :] = xc.astype(BF16)
            else:
                k_ref[c - n_q_chunks, rs, :] = xc.astype(BF16)


def _project(x, sh, sc, g, wm, wvt, mavg, gn, cos, sin):
    bsz, n, _ = x.shape
    tm = min(PROJ_ROWS, n)
    rope = cos is not None
    n_norm = gn.shape[1]
    ncol = wm.shape[1]
    kv_w = N_KV_HEADS * HEAD_DIM
    q_chunks = D_MODEL // LANES
    mod_rows = sh.shape[0]
    mod_spec = pl.BlockSpec((None, 1, D_MODEL),
                            (lambda i, b: (b, 0, 0)) if mod_rows == bsz else (lambda i, b: (0, 0, 0)))
    in_specs = [pl.BlockSpec((None, tm, D_MODEL), lambda i, b: (b, i, 0)),
                mod_spec, mod_spec,
                _const_spec((1, D_MODEL)),
                _const_spec((D_MODEL, ncol)),
                _const_spec((kv_w, D_MODEL)),
                _const_spec(mavg.shape),
                _const_spec((1, n_norm))]
    args = [x, sh, sc, g, wm, wvt, mavg, gn]
    out_specs = [pl.BlockSpec((None, N_KV_HEADS, tm, LANES), lambda i, b: (b, 0, i, 0)),
                 pl.BlockSpec((None, N_KV_HEADS, HEAD_DIM, tm), lambda i, b: (b, 0, 0, i)),
                 pl.BlockSpec((None, tm, D_SSM), lambda i, b: (b, i, 0))]
    out_shape = [jax.ShapeDtypeStruct((bsz, N_KV_HEADS, n, LANES), BF16),
                 jax.ShapeDtypeStruct((bsz, N_KV_HEADS, HEAD_DIM, n), BF16),
                 jax.ShapeDtypeStruct((bsz, n, D_SSM), F32)]
    if rope:
        in_specs += [pl.BlockSpec((tm, LANES), lambda i, b: (i, 0))] * 2
        args += [cos, sin]
        out_specs = [pl.BlockSpec((None, q_chunks, tm, LANES), lambda i, b: (b, 0, i, 0))] + out_specs
        out_shape = [jax.ShapeDtypeStruct((bsz, q_chunks, n, LANES), BF16)] + out_shape
    outs = pl.pallas_call(
        functools.partial(_proj_body, n_norm=n_norm, rope=rope),
        grid=(n // tm, bsz),
        in_specs=in_specs,
        out_specs=out_specs,
        out_shape=out_shape,
        compiler_params=_params(2),
        name="proj_latent" if rope else "proj_ctx",
    )(*args)
    return outs if rope else [None] + list(outs)


def _attn_body(q_ref, qn_ref, kl_ref, kc_ref, vtl_ref, vtc_ref, o_ref, s0_scr, s1_scr, m_scr, o_scr):
    n_ctx = kc_ref.shape[1]
    n_lat = kl_ref.shape[1]
    key_blocks = ([(lo, min(ATTN_KEYS, n_ctx - lo)) for lo in range(0, n_ctx, ATTN_KEYS)]
                  + [(n_ctx + lo, ATTN_KEYS) for lo in range(0, n_lat, ATTN_KEYS)])
    tq = q_ref.shape[1]
    lane = lax.broadcasted_iota(jnp.int32, (tq, LANES), 1)
    head_lanes = (lane < HEAD_DIM, lane >= HEAD_DIM)
    s_bufs = (s0_scr, s1_scr)
    nt = (((1,), (1,)), ((), ()))
    reload_base = jnp.minimum(pl.program_id(1), 0)

    def slab_max(s):
        return jnp.max(s.reshape(s.shape[0] // SUBLANES, SUBLANES, tq), axis=0)

    def scores(qc, g, r, slot):
        s_scr = s_bufs[slot]
        qm = jnp.where(head_lanes[r % 2], qc, jnp.zeros_like(qc))
        sc = lax.dot_general(kc_ref[g], qm, nt, preferred_element_type=F32)
        sl = lax.dot_general(kl_ref[g], qm, nt, preferred_element_type=F32)
        s_scr[:n_ctx, :] = sc
        s_scr[n_ctx:, :] = sl
        m_scr[slot] = jnp.maximum(slab_max(sc), slab_max(sl))

    def finish(g, r, slot):
        s_scr = s_bufs[slot]
        m = jnp.max(m_scr[slot], axis=0, keepdims=True)
        m8 = jnp.broadcast_to(m, (SUBLANES, tq))
        acc = jnp.zeros((HEAD_DIM + ONES_ROWS, tq), F32)
        for lo, size in key_blocks:
            row = pl.multiple_of(reload_base + lo, SUBLANES)
            s = s_scr[pl.ds(row, size), :].reshape(size // SUBLANES, SUBLANES, tq)
            pb = jnp.exp2(s - m8).reshape(size, tq).astype(BF16)
            if lo < n_ctx:
                vt = vtc_ref[g, :, lo:lo + size]
            else:
                vt = vtl_ref[g, :, lo - n_ctx:lo - n_ctx + size]
            ones_rows = jnp.ones((ONES_ROWS, size), BF16)
            acc = acc + jnp.dot(jnp.concatenate([vt, ones_rows], axis=0), pb, preferred_element_type=F32)
        o_scr[KV_REP * g + r] = acc

    @pl.when(pl.program_id(1) == 0)
    def _():
        scores(q_ref[0], 0, 0, 0)

    def group(g, carry):
        for r in range(KV_REP):
            if r + 1 < KV_REP:
                scores(q_ref[2 * g + (r + 1) // 2], g, r + 1, (r + 1) % 2)
            else:
                last = g == N_KV_HEADS - 1
                qc = jnp.where(last, qn_ref[0], q_ref[2 * jnp.minimum(g + 1, N_KV_HEADS - 1)])
                scores(qc, jnp.where(last, 0, g + 1), 0, (r + 1) % 2)
            finish(g, r, r % 2)
        return carry

    lax.fori_loop(0, N_KV_HEADS, group, 0)
    o_all = o_scr[...]
    o_all = o_all[:, :HEAD_DIM, :] / o_all[:, HEAD_DIM:HEAD_DIM + 1, :]
    o_ref[...] = o_all.reshape(N_HEADS * HEAD_DIM, tq).T.astype(BF16)


def _attention(q, k_l, k_c, vt_l, vt_c):
    bsz, q_chunks, n, _ = q.shape
    n_ctx = k_c.shape[2]
    tq = ATTN_ROWS
    whole = lambda shape: pl.BlockSpec((None,) + shape, lambda b, j: (b,) + (0,) * len(shape))
    return pl.pallas_call(
        _attn_body,
        grid=(bsz, n // tq),
        in_specs=[pl.BlockSpec((None, q_chunks, tq, LANES), lambda b, j: (b, 0, j, 0)),
                  pl.BlockSpec((None, 1, tq, LANES), lambda b, j: (b, 0, jnp.minimum(j + 1, n // tq - 1), 0)),
                  whole((N_KV_HEADS, n, LANES)),
                  whole((N_KV_HEADS, n_ctx, LANES)),
                  whole((N_KV_HEADS, HEAD_DIM, n)),
                  whole((N_KV_HEADS, HEAD_DIM, n_ctx))],
        out_specs=pl.BlockSpec((None, tq, D_MODEL), lambda b, j: (b, j, 0)),
        out_shape=jax.ShapeDtypeStruct((bsz, n, D_MODEL), BF16),
        scratch_shapes=[pltpu.VMEM((n_ctx + n, tq), F32),
                        pltpu.VMEM((n_ctx + n, tq), F32),
                        pltpu.VMEM((2, SUBLANES, tq), F32),
                        pltpu.VMEM((N_HEADS, HEAD_DIM + ONES_ROWS, tq), F32)],
        compiler_params=_params(2),
        name="attn",
    )(q, q, k_l, k_c, vt_l, vt_c)


def _ssm_body(uc_ref, ul_ref, bm_ref, cm_ref, ar_ref, ai_ref, y_ref,
              ut_scr, yt_scr, st_scr, hr_scr, hi_scr, *, n_ctx_tiles):
    d = pl.program_id(0)
    i = pl.program_id(2)
    steps = uc_ref.shape[1]
    half_w = N_STATE // 2
    n_slabs = D_SSM // LANES

    @pl.when(i == 0)
    def _():
        hr_scr[...] = jnp.zeros_like(hr_scr)
        hi_scr[...] = jnp.zeros_like(hi_scr)

    u = jnp.where(i < n_ctx_tiles, uc_ref[...], ul_ref[...])
    for b in range(SSM_BATCH):
        for c in range(n_slabs):
            ut_scr[c, pl.ds(b, steps, stride=SSM_BATCH), :] = u[b, :, LANES * c:LANES * (c + 1)]
    ub = jnp.concatenate([ut_scr[c] for c in range(n_slabs)], axis=1).astype(BF16)
    for k in range(2):
        r = jnp.dot(ub[:, 256 * k:256 * (k + 1)], bm_ref[k], preferred_element_type=F32)
        st_scr[:, half_w * k:half_w * (k + 1)] = r[:, :half_w]
        st_scr[:, N_STATE + half_w * k:N_STATE + half_w * (k + 1)] = r[:, half_w:]

    def scan(forward):
        for c in range(N_STATE // SCAN_LANES):
            re = slice(SCAN_LANES * c, SCAN_LANES * (c + 1))
            im = slice(N_STATE + SCAN_LANES * c, N_STATE + SCAN_LANES * (c + 1))
            a_r = ar_ref[:, re]
            a_i = ai_ref[:, re]

            def step(j, carry):
                hr, hi = carry
                t = j if forward else steps - 1 - j
                row = pl.multiple_of(t * SSM_BATCH, SSM_BATCH)
                br = st_scr[pl.ds(row, SSM_BATCH), re]
                bi = st_scr[pl.ds(row, SSM_BATCH), im]
                hr, hi = a_r * hr - a_i * hi + br, a_r * hi + a_i * hr + bi
                st_scr[pl.ds(row, SSM_BATCH), re] = hr
                st_scr[pl.ds(row, SSM_BATCH), im] = hi
                return hr, hi

            hr, hi = lax.fori_loop(0, steps, step, (hr_scr[:, re], hi_scr[:, re]), unroll=SCAN_UNROLL)
            hr_scr[:, re] = hr
            hi_scr[:, re] = hi

    @pl.when(d == 0)
    def _():
        scan(True)

    @pl.when(d == 1)
    def _():
        scan(False)

    @pl.when(i >= n_ctx_tiles)
    def _():
        for k in range(2):
            y = (jnp.dot(st_scr[:, half_w * k:half_w * (k + 1)].astype(BF16), cm_ref[k, 0],
                         preferred_element_type=F32)
                 + jnp.dot(st_scr[:, N_STATE + half_w * k:N_STATE + half_w * (k + 1)].astype(BF16), cm_ref[k, 1],
                           preferred_element_type=F32))
            for c in range(2):
                yt_scr[2 * k + c] = y[:, LANES * c:LANES * (c + 1)]
        for b in range(SSM_BATCH):
            y_ref[b] = jnp.concatenate(
                [yt_scr[c, pl.ds(b, steps, stride=SSM_BATCH), :] for c in range(n_slabs)], axis=1)


def _ssm(u_c, u_l, bm, cm, a_re, a_im):
    bsz, n_ctx, _ = u_c.shape
    n = u_l.shape[1]
    t = SSM_STEPS
    nct = n_ctx // t
    nlt = n // t
    rows = t * SSM_BATCH

    def ctx_tile(d, i):
        return jnp.where(d == 0, jnp.minimum(i, nct - 1), jnp.maximum(nct - 1 - i, 0))

    def lat_tile(d, i):
        j = jnp.maximum(i - nct, 0)
        return jnp.where(d == 0, j, nlt - 1 - j)

    once = pl.Buffered(1)
    return pl.pallas_call(
        functools.partial(_ssm_body, n_ctx_tiles=nct),
        grid=(2, bsz // SSM_BATCH, nct + nlt),
        in_specs=[pl.BlockSpec((SSM_BATCH, t, D_SSM), lambda d, bh, i: (bh, ctx_tile(d, i), 0)),
                  pl.BlockSpec((SSM_BATCH, t, D_SSM), lambda d, bh, i: (bh, lat_tile(d, i), 0)),
                  pl.BlockSpec((None, 2, 256, N_STATE), lambda d, bh, i: (d, 0, 0, 0), pipeline_mode=once),
                  pl.BlockSpec((None, 2, 2, N_STATE // 2, 256), lambda d, bh, i: (d, 0, 0, 0, 0),
                               pipeline_mode=once),
                  pl.BlockSpec((None, SUBLANES, N_STATE), lambda d, bh, i: (d, 0, 0), pipeline_mode=once),
                  pl.BlockSpec((None, SUBLANES, N_STATE), lambda d, bh, i: (d, 0, 0), pipeline_mode=once)],
        out_specs=pl.BlockSpec((None, SSM_BATCH, t, D_SSM), lambda d, bh, i: (d, bh, lat_tile(d, i), 0)),
        out_shape=jax.ShapeDtypeStruct((2, bsz, n, D_SSM), F32),
        scratch_shapes=[pltpu.VMEM((D_SSM // LANES, rows, LANES), F32),
                        pltpu.VMEM((D_SSM // LANES, rows, LANES), F32),
                        pltpu.VMEM((rows, 2 * N_STATE), F32),
                        pltpu.VMEM((SSM_BATCH, N_STATE), F32),
                        pltpu.VMEM((SSM_BATCH, N_STATE), F32)],
        compiler_params=_params(3),
        name="ssm",
    )(u_c, u_l, bm, cm, a_re, a_im)


def _merge_body(x_ref, yf_ref, yb_ref, u_ref, at_ref,
                sh1_ref, sc1_ref, g1_ref, sh2_ref, sc2_ref, g2_ref, n1_ref, n2_ref,
                wg_ref, dv_ref, wglu_ref, bglu_ref, wba_ref, wbs_ref, wo_ref,
                w1_ref, b1_ref, w2_ref, b2_ref, o_ref):
    x = x_ref[...]
    hb = _rms_modulate(x, n1_ref[...], sc1_ref[...], sh1_ref[...]).astype(BF16)
    gates = jnp.dot(hb, wg_ref[...], preferred_element_type=F32)
    gate_a = jax.nn.sigmoid(gates[:, :D_MODEL])
    gate_s = jax.nn.sigmoid(gates[:, D_MODEL:])

    y = yf_ref[...] + yb_ref[...] + dv_ref[...] * u_ref[...]
    y = jax.nn.gelu(y)
    z = jnp.dot(y.astype(BF16), wglu_ref[...], preferred_element_type=F32) + bglu_ref[...]
    y = y * jax.nn.sigmoid(z)

    merged = (gate_a * jnp.dot(at_ref[...], wba_ref[...], preferred_element_type=F32)
              + gate_s * jnp.dot(y.astype(BF16), wbs_ref[...], preferred_element_type=F32))
    mix = jnp.dot(merged.astype(BF16), wo_ref[...], preferred_element_type=F32)
    x1 = x + g1_ref[...] * mix

    h2 = _rms_modulate(x1, n2_ref[...], sc2_ref[...], sh2_ref[...]).astype(BF16)
    acc = jnp.zeros_like(x1)
    for c in range(D_FF // FF_CHUNK):
        cols = slice(FF_CHUNK * c, FF_CHUNK * (c + 1))
        hid = jnp.dot(h2, w1_ref[:, cols], preferred_element_type=F32) + b1_ref[:, cols]
        hid = jnp.square(jnp.maximum(hid, 0.0))
        acc = acc + jnp.dot(hid.astype(BF16), w2_ref[cols, :], preferred_element_type=F32)
    o_ref[...] = x1 + g2_ref[...] * (acc + b2_ref[...])


def _merge(x, y2, u2, attn, mods, n1, n2, wg, dvec, wglu, bglu, wba, wbs, wo, w1, b1, w2, b2):
    bsz, n, _ = x.shape
    tm = MERGE_ROWS
    row_spec = pl.BlockSpec((None, tm, D_MODEL), lambda b, i: (b, i, 0))
    mod_spec = pl.BlockSpec((None, 1, D_MODEL), lambda b, i: (b, 0, 0))
    in_specs = [row_spec,
                pl.BlockSpec((None, None, tm, D_SSM), lambda b, i: (0, b, i, 0)),
                pl.BlockSpec((None, None, tm, D_SSM), lambda b, i: (1, b, i, 0)),
                pl.BlockSpec((None, tm, D_SSM), lambda b, i: (b, i, 0)),
                row_spec] + [mod_spec] * 6
    consts = [n1, n2, wg, dvec, wglu, bglu, wba, wbs, wo, w1, b1, w2, b2]
    in_specs += [_const_spec(a.shape) for a in consts]
    return pl.pallas_call(
        _merge_body,
        grid=(bsz, n // tm),
        in_specs=in_specs,
        out_specs=row_spec,
        out_shape=jax.ShapeDtypeStruct(x.shape, x.dtype),
        compiler_params=_params(2),
        name="merge",
    )(x, y2, y2, u2, attn, *mods, *consts)


def _rope_tables(n_tok):
    half = ROPE_AXIS_DIM // 2
    pos = jnp.arange(n_tok, dtype=jnp.int32)
    row = (pos // GRID_W).astype(F32)
    col = (pos % GRID_W).astype(F32)
    inv_freq = ROPE_THETA ** (-jnp.arange(half, dtype=F32) / half)
    ang_r = row[:, None] * inv_freq
    ang_c = col[:, None] * inv_freq
    cos = jnp.concatenate([jnp.cos(ang_r)] * 2 + [jnp.cos(ang_c)] * 2, axis=-1)
    sin = jnp.concatenate([-jnp.sin(ang_r), jnp.sin(ang_r), -jnp.sin(ang_c), jnp.sin(ang_c)], axis=-1)
    reps = LANES // HEAD_DIM
    return jnp.tile(cos, (1, reps)), jnp.tile(sin, (1, reps))


def kernel(x, c, ctx, c_ctx, w_mod, b_mod, norm1_g, norm2_g, w_in, q_norm_g, k_norm_g,
           ssm_lambda_re, ssm_lambda_im, ssm_log_dt, ssm_b_re, ssm_b_im, ssm_c_re, ssm_c_im,
           ssm_d, w_glu, b_glu, w_br_attn, w_br_ssm, w_out, w_mlp1, b_mlp1, w_mlp2, b_mlp2):
    bsz, n_tok, d_model = x.shape
    n_ctx = ctx.shape[1]
    assert d_model == D_MODEL and w_mod.shape[0] == 1, "single-layer block of width 1024 only"
    assert n_tok % PROJ_ROWS == 0 and n_ctx % min(PROJ_ROWS, n_ctx) == 0 and n_ctx % SSM_STEPS == 0
    assert bsz % SSM_BATCH == 0 and n_tok % GRID_W == 0

    pad = (-(bsz + 1)) % SUBLANES
    c_rows = jnp.concatenate([c, c_ctx[None, :], jnp.zeros((pad, D_MODEL), F32)], axis=0)
    mod = _modulation(c_rows, w_mod[0], b_mod[0][None, :])
    sh1, sc1, g1, sh2, sc2, g2 = [m[:, None, :] for m in jnp.split(mod, 6, axis=-1)]
    lat = lambda m: m[:bsz]
    cx = lambda m: m[bsz:bsz + 1]

    w = w_in[0]
    w_k = w[:, OFF_K:OFF_V].reshape(D_MODEL, N_KV_HEADS, 1, HEAD_DIM)
    w_kdup = jnp.broadcast_to(w_k, (D_MODEL, N_KV_HEADS, 2, HEAD_DIM)).reshape(D_MODEL, 2 * OFF_V - 2 * OFF_K)
    w_u = w[:, OFF_U:OFF_GA]
    wm_lat = jnp.concatenate([w[:, :OFF_K], w_kdup, w_u], axis=1).astype(BF16)
    wm_ctx = jnp.concatenate([w_kdup, w_u], axis=1).astype(BF16)
    w_vt = w[:, OFF_V:OFF_U].T.astype(BF16)
    w_gate = w[:, OFF_GA:].astype(BF16)
    mavg = jnp.kron(jnp.eye(MXU_TILE // HEAD_DIM, dtype=F32),
                    jnp.full((HEAD_DIM, HEAD_DIM), 1.0 / HEAD_DIM, F32)).astype(BF16)
    gk = jnp.tile(k_norm_g[0], 2 * N_KV_HEADS)
    gq = jnp.tile(q_norm_g[0], N_HEADS) * (HEAD_DIM ** -0.5 * LOG2_E)
    gn_lat = jnp.concatenate([gq, gk])[None, :]
    gn_ctx = gk[None, :]
    cos, sin = _rope_tables(n_tok)
    n1 = norm1_g[0][None, :]

    q, k_l, vt_l, u_l = _project(x, lat(sh1), lat(sc1), n1, wm_lat, w_vt, mavg, gn_lat, cos, sin)
    _, k_c, vt_c, u_c = _project(ctx, cx(sh1), cx(sc1), n1, wm_ctx, w_vt, mavg, gn_ctx, None, None)

    attn = _attention(q, k_l, k_c, vt_l, vt_c)

    a_re, a_im, bb_re, bb_im = _discretize(ssm_lambda_re[0], ssm_lambda_im[0], ssm_log_dt[0],
                                           ssm_b_re[0], ssm_b_im[0])
    bm, cm, a_re8, a_im8 = _ssm_matrices(a_re, a_im, bb_re, bb_im, ssm_c_re[0], ssm_c_im[0])
    y = _ssm(u_c, u_l, bm, cm, a_re8, a_im8)

    mods = [lat(m) for m in (sh1, sc1, g1, sh2, sc2, g2)]
    return _merge(x, y, u_l, attn, mods, n1, norm2_g[0][None, :],
                  w_gate, ssm_d[0][None, :], w_glu[0].astype(BF16), b_glu[0][None, :],
                  w_br_attn[0].astype(BF16), w_br_ssm[0].astype(BF16), w_out[0].astype(BF16),
                  w_mlp1[0].astype(BF16), b_mlp1[0][None, :], w_mlp2[0].astype(BF16), b_mlp2[0][None, :])
```

```python
import functools

import jax
import jax.numpy as jnp
from jax import lax
from jax.experimental import pallas as pl
from jax.experimental.pallas import tpu as pltpu

F32 = jnp.float32
BF16 = jnp.bfloat16

D_MODEL = 1024
N_HEADS = 16
N_KV_HEADS = 4
HEAD_DIM = 64
KV_REP = N_HEADS // N_KV_HEADS
GRID_W = 64
ROPE_THETA = 10000.0
ROPE_AXIS_DIM = HEAD_DIM // 2
D_SSM = 512
SSM_GROUP = 16
N_SSM_GROUPS = 32
SSM_STATE = 64
N_STATE = N_SSM_GROUPS * SSM_STATE
D_FF = 4 * D_MODEL
NORM_EPS = 1e-6
OFF_K = 1024
OFF_V = 1280
OFF_U = 1536
OFF_GA = 2048

LANES = 128
SUBLANES = 8
MXU_TILE = 256
LOG2_E = 1.4426950408889634
VMEM_LIMIT_BYTES = 56 * 1024 * 1024

PROJ_ROWS = 1024
PROJ_SUB_ROWS = 256
ATTN_ROWS = 256
ATTN_KEYS = 256
ONES_ROWS = 16
MERGE_ROWS = 512
SSM_CHUNK = 16
SSM_ROW_PAD = 4
FF_CHUNK = 1024


def _const_spec(shape):
    nd = len(shape)
    return pl.BlockSpec(shape, lambda *_: (0,) * nd, pipeline_mode=pl.Buffered(1))


def _params(n_axes, vmem=VMEM_LIMIT_BYTES, flags=None):
    return pltpu.CompilerParams(dimension_semantics=("arbitrary",) * n_axes, vmem_limit_bytes=vmem,
                                flags=flags)


def _mod_body(c_ref, w_ref, b_ref, o_ref):
    c = c_ref[...]
    s = c * jax.nn.sigmoid(c)
    o_ref[...] = jnp.dot(s, w_ref[...], preferred_element_type=F32) + b_ref[...]


def _modulation(c_rows, w_mod, b_mod):
    rows = c_rows.shape[0]
    ncol = w_mod.shape[1]
    blk = D_MODEL
    return pl.pallas_call(
        _mod_body,
        grid=(ncol // blk,),
        in_specs=[pl.BlockSpec((rows, D_MODEL), lambda j: (0, 0)),
                  pl.BlockSpec((D_MODEL, blk), lambda j: (0, j)),
                  pl.BlockSpec((1, blk), lambda j: (0, j))],
        out_specs=pl.BlockSpec((rows, blk), lambda j: (0, j)),
        out_shape=jax.ShapeDtypeStruct((rows, ncol), F32),
        compiler_params=_params(1),
        name="mod",
    )(c_rows, w_mod, b_mod)


def _disc_body(lr_ref, li_ref, ldt_ref, br_ref, bi_ref, ar_ref, ai_ref, bbr_ref, bbi_ref):
    lr = lr_ref[...]
    li = li_ref[...]
    dt = jnp.exp(ldt_ref[...])
    mag = jnp.exp(lr * dt)
    ar = mag * jnp.cos(li * dt)
    ai = mag * jnp.sin(li * dt)
    nr = ar - 1.0
    den = lr * lr + li * li
    cr = (nr * lr + ai * li) / den
    ci = (ai * lr - nr * li) / den
    br = br_ref[...]
    bi = bi_ref[...]
    ar_ref[...] = ar
    ai_ref[...] = ai
    bbr_ref[...] = cr * br - ci * bi
    bbi_ref[...] = cr * bi + ci * br


def _discretize(lam_re, lam_im, log_dt, b_re, b_im):
    rows = 2 * N_SSM_GROUPS
    cols = SSM_STATE * SSM_GROUP
    rep = lambda a: jnp.repeat(a.reshape(rows, SSM_STATE), SSM_GROUP, axis=1)
    ldt = jnp.broadcast_to(log_dt.reshape(rows, 1), (rows, cols))
    spec = pl.BlockSpec((rows, cols), lambda: (0, 0))
    sds = jax.ShapeDtypeStruct((rows, cols), F32)
    ar, ai, bbr, bbi = pl.pallas_call(
        _disc_body,
        in_specs=[spec] * 5,
        out_specs=[spec] * 4,
        out_shape=[sds] * 4,
        name="disc",
    )(rep(lam_re), rep(lam_im), ldt, b_re.reshape(rows, cols), b_im.reshape(rows, cols))
    a_re = ar[:, ::SSM_GROUP].reshape(2, N_STATE)
    a_im = ai[:, ::SSM_GROUP].reshape(2, N_STATE)
    shape = (2, N_SSM_GROUPS, SSM_STATE, SSM_GROUP)
    return a_re, a_im, bbr.reshape(shape), bbi.reshape(shape)


def _rms_modulate(x, g, sc, sh):
    ms = jnp.mean(x * x, axis=-1, keepdims=True)
    xn = x * lax.rsqrt(ms + NORM_EPS) * g
    return xn * (1.0 + sc) + sh


def _proj_body(*refs, n_norm, rope):
    if rope:
        (x_ref, sh_ref, sc_ref, g_ref, wm_ref, wvt_ref, mavg_ref, gn_ref, cos_ref, sin_ref,
         q_ref, k_ref, vt_ref, u_ref) = refs
    else:
        (x_ref, sh_ref, sc_ref, g_ref, wm_ref, wvt_ref, mavg_ref, gn_ref,
         k_ref, vt_ref, u_ref) = refs
    rows = x_ref.shape[0]
    sub = min(rows, PROJ_SUB_ROWS)
    mavg = mavg_ref[...]
    width = mavg.shape[0]
    n_q_chunks = n_norm // LANES - k_ref.shape[0]
    if rope:
        lane = lax.broadcasted_iota(jnp.int32, (sub, LANES), 1)
        half = ROPE_AXIS_DIM // 2
        first_half = (lane % ROPE_AXIS_DIM) < half
    for r0 in range(0, rows, sub):
        rs = slice(r0, r0 + sub)
        h = _rms_modulate(x_ref[rs, :], g_ref[...], sc_ref[...], sh_ref[...])
        hb = h.astype(BF16)
        p = jnp.dot(hb, wm_ref[...], preferred_element_type=F32)
        qk = p[:, :n_norm]
        u_ref[rs, :] = p[:, n_norm:]
        sq = (qk * qk).astype(BF16)
        msq = jnp.concatenate(
            [jnp.dot(sq[:, width * j:width * (j + 1)], mavg, preferred_element_type=F32)
             for j in range(n_norm // width)], axis=1)
        vt = lax.dot_general(wvt_ref[...], hb, (((1,), (1,)), ((), ())), preferred_element_type=F32)
        for g in range(N_KV_HEADS):
            vt_ref[g, :, rs] = vt[HEAD_DIM * g:HEAD_DIM * (g + 1), :].astype(BF16)
        qkn = qk * lax.rsqrt(msq + NORM_EPS) * gn_ref[...]
        for c in range(n_norm // LANES):
            xc = qkn[:, LANES * c:LANES * (c + 1)]
            if rope:
                partner = jnp.where(first_half, pltpu.roll(xc, LANES - half, 1), pltpu.roll(xc, half, 1))
                xc = xc * cos_ref[rs, :] + partner * sin_ref[rs, :]
            if c < n_q_chunks:
                q_ref[c, rs, :] = xc.astype(BF16)
            else:
                k_ref[c - n_q_chunks, rs, :] = xc.astype(BF16)


def _project(x, sh, sc, g, wm, wvt, mavg, gn, cos, sin):
    bsz, n, _ = x.shape
    tm = min(PROJ_ROWS, n)
    rope = cos is not None
    n_norm = gn.shape[1]
    ncol = wm.shape[1]
    kv_w = N_KV_HEADS * HEAD_DIM
    q_chunks = D_MODEL // LANES
    mod_rows = sh.shape[0]
    mod_spec = pl.BlockSpec((None, 1, D_MODEL),
                            (lambda i, b: (b, 0, 0)) if mod_rows == bsz else (lambda i, b: (0, 0, 0)))
    in_specs = [pl.BlockSpec((None, tm, D_MODEL), lambda i, b: (b, i, 0)),
                mod_spec, mod_spec,
                _const_spec((1, D_MODEL)),
                _const_spec((D_MODEL, ncol)),
                _const_spec((kv_w, D_MODEL)),
                _const_spec(mavg.shape),
                _const_spec((1, n_norm))]
    args = [x, sh, sc, g, wm, wvt, mavg, gn]
    out_specs = [pl.BlockSpec((None, N_KV_HEADS, tm, LANES), lambda i, b: (b, 0, i, 0)),
                 pl.BlockSpec((None, N_KV_HEADS, HEAD_DIM, tm), lambda i, b: (b, 0, 0, i)),
                 pl.BlockSpec((None, tm, D_SSM), lambda i, b: (b, i, 0))]
    out_shape = [jax.ShapeDtypeStruct((bsz, N_KV_HEADS, n, LANES), BF16),
                 jax.ShapeDtypeStruct((bsz, N_KV_HEADS, HEAD_DIM, n), BF16),
                 jax.ShapeDtypeStruct((bsz, n, D_SSM), F32)]
    if rope:
        in_specs += [pl.BlockSpec((tm, LANES), lambda i, b: (i, 0))] * 2
        args += [cos, sin]
        out_specs = [pl.BlockSpec((None, q_chunks, tm, LANES), lambda i, b: (b, 0, i, 0))] + out_specs
        out_shape = [jax.ShapeDtypeStruct((bsz, q_chunks, n, LANES), BF16)] + out_shape
    outs = pl.pallas_call(
        functools.partial(_proj_body, n_norm=n_norm, rope=rope),
        grid=(n // tm, bsz),
        in_specs=in_specs,
        out_specs=out_specs,
        out_shape=out_shape,
        compiler_params=_params(2),
        name="proj_latent" if rope else "proj_ctx",
    )(*args)
    return outs if rope else [None] + list(outs)


def _attn_body(q_ref, qn_ref, kl_ref, kc_ref, vtl_ref, vtc_ref, o_ref, s0_scr, s1_scr, m_scr, o_scr):
    n_ctx = kc_ref.shape[1]
    n_lat = kl_ref.shape[1]
    key_blocks = ([(lo, min(ATTN_KEYS, n_ctx - lo)) for lo in range(0, n_ctx, ATTN_KEYS)]
                  + [(n_ctx + lo, ATTN_KEYS) for lo in range(0, n_lat, ATTN_KEYS)])
    tq = q_ref.shape[1]
    lane = lax.broadcasted_iota(jnp.int32, (tq, LANES), 1)
    head_lanes = (lane < HEAD_DIM, lane >= HEAD_DIM)
    s_bufs = (s0_scr, s1_scr)
    nt = (((1,), (1,)), ((), ()))
    reload_base = jnp.minimum(pl.program_id(1), 0)

    def slab_max(s):
        return jnp.max(s.reshape(s.shape[0] // SUBLANES, SUBLANES, tq), axis=0)

    def scores(qc, g, r, slot):
        s_scr = s_bufs[slot]
        qm = jnp.where(head_lanes[r % 2], qc, jnp.zeros_like(qc))
        sc = lax.dot_general(kc_ref[g], qm, nt, preferred_element_type=F32)
        sl = lax.dot_general(kl_ref[g], qm, nt, preferred_element_type=F32)
        s_scr[:n_ctx, :] = sc
        s_scr[n_ctx:, :] = sl
        m_scr[slot] = jnp.maximum(slab_max(sc), slab_max(sl))

    def finish(g, r, slot):
        s_scr = s_bufs[slot]
        m = jnp.max(m_scr[slot], axis=0, keepdims=True)
        m8 = jnp.broadcast_to(m, (SUBLANES, tq))
        acc = jnp.zeros((HEAD_DIM + ONES_ROWS, tq), F32)
        for lo, size in key_blocks:
            row = pl.multiple_of(reload_base + lo, SUBLANES)
            s = s_scr[pl.ds(row, size), :].reshape(size // SUBLANES, SUBLANES, tq)
            pb = jnp.exp2(s - m8).reshape(size, tq).astype(BF16)
            if lo < n_ctx:
                vt = vtc_ref[g, :, lo:lo + size]
            else:
                vt = vtl_ref[g, :, lo - n_ctx:lo - n_ctx + size]
            ones_rows = jnp.ones((ONES_ROWS, size), BF16)
            acc = acc + jnp.dot(jnp.concatenate([vt, ones_rows], axis=0), pb, preferred_element_type=F32)
        o_scr[KV_REP * g + r] = acc

    @pl.when(pl.program_id(1) == 0)
    def _():
        scores(q_ref[0], 0, 0, 0)

    def group(g, carry):
        for r in range(KV_REP):
            if r + 1 < KV_REP:
                scores(q_ref[2 * g + (r + 1) // 2], g, r + 1, (r + 1) % 2)
            else:
                last = g == N_KV_HEADS - 1
                qc = jnp.where(last, qn_ref[0], q_ref[2 * jnp.minimum(g + 1, N_KV_HEADS - 1)])
                scores(qc, jnp.where(last, 0, g + 1), 0, (r + 1) % 2)
            finish(g, r, r % 2)
        return carry

    lax.fori_loop(0, N_KV_HEADS, group, 0)
    o_all = o_scr[...]
    o_all = o_all[:, :HEAD_DIM, :] / o_all[:, HEAD_DIM:HEAD_DIM + 1, :]
    o_ref[...] = o_all.reshape(N_HEADS * HEAD_DIM, tq).T.astype(BF16)


def _attention(q, k_l, k_c, vt_l, vt_c):
    bsz, q_chunks, n, _ = q.shape
    n_ctx = k_c.shape[2]
    tq = ATTN_ROWS
    whole = lambda shape: pl.BlockSpec((None,) + shape, lambda b, j: (b,) + (0,) * len(shape))
    return pl.pallas_call(
        _attn_body,
        grid=(bsz, n // tq),
        in_specs=[pl.BlockSpec((None, q_chunks, tq, LANES), lambda b, j: (b, 0, j, 0)),
                  pl.BlockSpec((None, 1, tq, LANES), lambda b, j: (b, 0, jnp.minimum(j + 1, n // tq - 1), 0)),
                  whole((N_KV_HEADS, n, LANES)),
                  whole((N_KV_HEADS, n_ctx, LANES)),
                  whole((N_KV_HEADS, HEAD_DIM, n)),
                  whole((N_KV_HEADS, HEAD_DIM, n_ctx))],
        out_specs=pl.BlockSpec((None, tq, D_MODEL), lambda b, j: (b, j, 0)),
        out_shape=jax.ShapeDtypeStruct((bsz, n, D_MODEL), BF16),
        scratch_shapes=[pltpu.VMEM((n_ctx + n, tq), F32),
                        pltpu.VMEM((n_ctx + n, tq), F32),
                        pltpu.VMEM((2, SUBLANES, tq), F32),
                        pltpu.VMEM((N_HEADS, HEAD_DIM + ONES_ROWS, tq), F32)],
        compiler_params=_params(2),
        name="attn",
    )(q, q, k_l, k_c, vt_l, vt_c)


def _chunk_matrices(a_re, a_im, bb_re, bb_im, c_re, c_im):
    hp = lax.Precision.HIGHEST
    g, p, h, cs = N_SSM_GROUPS, SSM_STATE, SSM_GROUP, SSM_CHUNK
    cmul = lambda xr, xi, yr, yi: (xr * yr - xi * yi, xr * yi + xi * yr)
    ar, ai = a_re.reshape(2, g, p), a_im.reshape(2, g, p)
    powers = [(jnp.ones_like(ar), jnp.zeros_like(ar))]
    for _ in range(cs):
        powers.append(cmul(*powers[-1], ar, ai))
    pr = jnp.stack([q[0] for q in powers])
    pi = jnp.stack([q[1] for q in powers])

    wr, wi = cmul(c_re[:, None], c_im[:, None],
                  pr[:cs].transpose(1, 0, 2, 3)[:, :, :, None, :], pi[:cs].transpose(1, 0, 2, 3)[:, :, :, None, :])
    kern = (jnp.einsum('dtghp,dgpk->dtghk', wr, bb_re, precision=hp)
            - jnp.einsum('dtghp,dgpk->dtghk', wi, bb_im, precision=hp))
    s_idx = jnp.arange(cs)[:, None]
    t_idx = jnp.arange(cs)[None, :]
    causal = (t_idx >= s_idx)[:, :, None, None, None]
    anti = (s_idx >= t_idx)[:, :, None, None, None]
    m = (jnp.where(causal, kern[0][jnp.clip(t_idx - s_idx, 0, cs - 1)], 0.0)
         + jnp.where(anti, kern[1][jnp.clip(s_idx - t_idx, 0, cs - 1)], 0.0))
    m = m.transpose(2, 0, 4, 1, 3).reshape(g, cs * h, cs * h)

    steps = jnp.arange(cs)
    in_f = cmul(pr[cs - 1 - steps, 0][..., None], pi[cs - 1 - steps, 0][..., None], bb_re[0][None], bb_im[0][None])
    in_b = cmul(pr[steps, 1][..., None], pi[steps, 1][..., None], bb_re[1][None], bb_im[1][None])
    rows = lambda z: z.transpose(1, 0, 3, 2).reshape(g, cs * h, p)
    b_in = jnp.concatenate([rows(in_f[0]), rows(in_f[1]), rows(in_b[0]), rows(in_b[1]),
                            rows(in_f[1]), rows(in_f[0]), rows(in_b[1]), rows(in_b[0])], axis=-1)

    c_gp = lambda c: c.transpose(0, 2, 1)[:, :, None, :]
    p_gp = lambda q: q.transpose(1, 2, 0)[..., None]
    out_f = cmul(c_gp(c_re[0]), c_gp(c_im[0]), p_gp(pr[1 + steps, 0]), p_gp(pi[1 + steps, 0]))
    out_b = cmul(c_gp(c_re[1]), c_gp(c_im[1]), p_gp(pr[cs - steps, 1]), p_gp(pi[cs - steps, 1]))
    cols = lambda z: z.reshape(g, p, cs * h)
    c_out = jnp.concatenate([cols(out_f[0]), cols(-out_f[1]), cols(out_b[0]), cols(-out_b[1])], axis=1)

    a1 = jnp.concatenate([pr[cs], pr[cs]], axis=-1)
    a2 = jnp.concatenate([-pi[cs], pi[cs]], axis=-1)
    return b_in.astype(BF16), jnp.concatenate([m, c_out], axis=1).astype(BF16), a1, a2


def _ssm_chunk_body(uf_ref, bin_ref, mc_ref, a1_ref, a2_ref, y_ref, sf_scr, sb_scr, xf_scr, xb_scr,
                    *, n_ctx_chunks):
    n_groups, nc, _ = uf_ref.shape
    pitch = sf_scr.shape[0] // n_groups
    for g in range(n_groups):
        r = jnp.dot(uf_ref[g], bin_ref[g], preferred_element_type=F32)
        for k, scr in enumerate((sf_scr, sb_scr, xf_scr, xb_scr)):
            scr[g * pitch:g * pitch + nc, :] = r[:, LANES * k:LANES * (k + 1)]

    def carry_states(scr, swapped_scr, a1, a2, chunk_of):
        def step(j, carry):
            state, mirror = carry
            rows = pl.ds(chunk_of(j), n_groups, stride=pitch)
            local = scr[rows, :]
            local_mirror = swapped_scr[rows, :]
            scr[rows, :] = state
            return state * a1 + mirror * a2 + local, mirror * a1 - state * a2 + local_mirror
        zeros = jnp.zeros((n_groups, LANES), F32)
        lax.fori_loop(0, nc, step, (zeros, zeros))

    carry_states(sf_scr, xf_scr, a1_ref[0], a2_ref[0], lambda j: j)
    carry_states(sb_scr, xb_scr, a1_ref[1], a2_ref[1],
                 lambda j: jnp.where(j < n_ctx_chunks, n_ctx_chunks - 1 - j, nc - 1 - (j - n_ctx_chunks)))

    for g in range(n_groups):
        lhs = jnp.concatenate([uf_ref[g],
                               sf_scr[g * pitch:g * pitch + nc, :].astype(BF16),
                               sb_scr[g * pitch:g * pitch + nc, :].astype(BF16)], axis=1)
        y_ref[g] = jnp.dot(lhs, mc_ref[g], preferred_element_type=F32)


def _ssm_chunked(u_c, u_l, b_in, mc, a1, a2):
    bsz, n_ctx, _ = u_c.shape
    n = u_l.shape[1]
    g, h, cs = N_SSM_GROUPS, SSM_GROUP, SSM_CHUNK
    nc = (n_ctx + n) // cs
    uf = (jnp.concatenate([u_c, u_l], axis=1).reshape(bsz, nc, cs, g, h)
          .transpose(0, 3, 1, 2, 4).reshape(bsz, g, nc, cs * h).astype(BF16))
    y_flat = pl.pallas_call(
        functools.partial(_ssm_chunk_body, n_ctx_chunks=n_ctx // cs),
        grid=(bsz,),
        in_specs=[pl.BlockSpec((None, g, nc, cs * h), lambda b: (b, 0, 0, 0)),
                  _const_spec(b_in.shape), _const_spec(mc.shape),
                  _const_spec(a1.shape), _const_spec(a2.shape)],
        out_specs=pl.BlockSpec((None, g, nc, cs * h), lambda b: (b, 0, 0, 0)),
        out_shape=jax.ShapeDtypeStruct((bsz, g, nc, cs * h), F32),
        scratch_shapes=[pltpu.VMEM((g * (nc + SSM_ROW_PAD), LANES), F32)] * 4,
        compiler_params=_params(1),
        name="ssm",
    )(uf, b_in, mc, a1, a2)
    y = y_flat.reshape(bsz, g, nc, cs, h).transpose(0, 2, 3, 1, 4).reshape(bsz, nc * cs, D_SSM)
    return y[:, n_ctx:]


def _merge_body(x_ref, y_ref, u_ref, at_ref,
                sh1_ref, sc1_ref, g1_ref, sh2_ref, sc2_ref, g2_ref, n1_ref, n2_ref,
                wg_ref, dv_ref, wglu_ref, bglu_ref, wba_ref, wbs_ref, wo_ref,
                w1_ref, b1_ref, w2_ref, b2_ref, o_ref):
    x = x_ref[...]
    hb = _rms_modulate(x, n1_ref[...], sc1_ref[...], sh1_ref[...]).astype(BF16)
    gates = jnp.dot(hb, wg_ref[...], preferred_element_type=F32)
    gate_a = jax.nn.sigmoid(gates[:, :D_MODEL])
    gate_s = jax.nn.sigmoid(gates[:, D_MODEL:])

    y = y_ref[...] + dv_ref[...] * u_ref[...]
    y = jax.nn.gelu(y)
    z = jnp.dot(y.astype(BF16), wglu_ref[...], preferred_element_type=F32) + bglu_ref[...]
    y = y * jax.nn.sigmoid(z)

    merged = (gate_a * jnp.dot(at_ref[...], wba_ref[...], preferred_element_type=F32)
              + gate_s * jnp.dot(y.astype(BF16), wbs_ref[...], preferred_element_type=F32))
    mix = jnp.dot(merged.astype(BF16), wo_ref[...], preferred_element_type=F32)
    x1 = x + g1_ref[...] * mix

    h2 = _rms_modulate(x1, n2_ref[...], sc2_ref[...], sh2_ref[...]).astype(BF16)
    acc = jnp.zeros_like(x1)
    for c in range(D_FF // FF_CHUNK):
        cols = slice(FF_CHUNK * c, FF_CHUNK * (c + 1))
        hid = jnp.dot(h2, w1_ref[:, cols], preferred_element_type=F32) + b1_ref[:, cols]
        hid = jnp.square(jnp.maximum(hid, 0.0))
        acc = acc + jnp.dot(hid.astype(BF16), w2_ref[cols, :], preferred_element_type=F32)
    o_ref[...] = x1 + g2_ref[...] * (acc + b2_ref[...])


def _merge(x, y2, u2, attn, mods, n1, n2, wg, dvec, wglu, bglu, wba, wbs, wo, w1, b1, w2, b2):
    bsz, n, _ = x.shape
    tm = MERGE_ROWS
    row_spec = pl.BlockSpec((None, tm, D_MODEL), lambda b, i: (b, i, 0))
    mod_spec = pl.BlockSpec((None, 1, D_MODEL), lambda b, i: (b, 0, 0))
    in_specs = [row_spec,
                pl.BlockSpec((None, tm, D_SSM), lambda b, i: (b, i, 0)),
                pl.BlockSpec((None, tm, D_SSM), lambda b, i: (b, i, 0)),
                row_spec] + [mod_spec] * 6
    consts = [n1, n2, wg, dvec, wglu, bglu, wba, wbs, wo, w1, b1, w2, b2]
    in_specs += [_const_spec(a.shape) for a in consts]
    return pl.pallas_call(
        _merge_body,
        grid=(bsz, n // tm),
        in_specs=in_specs,
        out_specs=row_spec,
        out_shape=jax.ShapeDtypeStruct(x.shape, x.dtype),
        compiler_params=_params(2),
        name="merge",
    )(x, y2, u2, attn, *mods, *consts)


def _rope_tables(n_tok):
    half = ROPE_AXIS_DIM // 2
    pos = jnp.arange(n_tok, dtype=jnp.int32)
    row = (pos // GRID_W).astype(F32)
    col = (pos % GRID_W).astype(F32)
    inv_freq = ROPE_THETA ** (-jnp.arange(half, dtype=F32) / half)
    ang_r = row[:, None] * inv_freq
    ang_c = col[:, None] * inv_freq
    cos = jnp.concatenate([jnp.cos(ang_r)] * 2 + [jnp.cos(ang_c)] * 2, axis=-1)
    sin = jnp.concatenate([-jnp.sin(ang_r), jnp.sin(ang_r), -jnp.sin(ang_c), jnp.sin(ang_c)], axis=-1)
    reps = LANES // HEAD_DIM
    return jnp.tile(cos, (1, reps)), jnp.tile(sin, (1, reps))


def kernel(x, c, ctx, c_ctx, w_mod, b_mod, norm1_g, norm2_g, w_in, q_norm_g, k_norm_g,
           ssm_lambda_re, ssm_lambda_im, ssm_log_dt, ssm_b_re, ssm_b_im, ssm_c_re, ssm_c_im,
           ssm_d, w_glu, b_glu, w_br_attn, w_br_ssm, w_out, w_mlp1, b_mlp1, w_mlp2, b_mlp2):
    bsz, n_tok, d_model = x.shape
    n_ctx = ctx.shape[1]
    assert d_model == D_MODEL and w_mod.shape[0] == 1, "single-layer block of width 1024 only"
    assert n_tok % PROJ_ROWS == 0 and n_ctx % min(PROJ_ROWS, n_ctx) == 0 and n_tok % MERGE_ROWS == 0
    assert n_ctx % SSM_CHUNK == 0 and n_tok % GRID_W == 0 and n_tok % ATTN_ROWS == 0

    pad = (-(bsz + 1)) % SUBLANES
    c_rows = jnp.concatenate([c, c_ctx[None, :], jnp.zeros((pad, D_MODEL), F32)], axis=0)
    mod = _modulation(c_rows, w_mod[0], b_mod[0][None, :])
    sh1, sc1, g1, sh2, sc2, g2 = [m[:, None, :] for m in jnp.split(mod, 6, axis=-1)]
    lat = lambda m: m[:bsz]
    cx = lambda m: m[bsz:bsz + 1]

    w = w_in[0]
    w_k = w[:, OFF_K:OFF_V].reshape(D_MODEL, N_KV_HEADS, 1, HEAD_DIM)
    w_kdup = jnp.broadcast_to(w_k, (D_MODEL, N_KV_HEADS, 2, HEAD_DIM)).reshape(D_MODEL, 2 * OFF_V - 2 * OFF_K)
    w_u = w[:, OFF_U:OFF_GA]
    wm_lat = jnp.concatenate([w[:, :OFF_K], w_kdup, w_u], axis=1).astype(BF16)
    wm_ctx = jnp.concatenate([w_kdup, w_u], axis=1).astype(BF16)
    w_vt = w[:, OFF_V:OFF_U].T.astype(BF16)
    w_gate = w[:, OFF_GA:].astype(BF16)
    mavg = jnp.kron(jnp.eye(MXU_TILE // HEAD_DIM, dtype=F32),
                    jnp.full((HEAD_DIM, HEAD_DIM), 1.0 / HEAD_DIM, F32)).astype(BF16)
    gk = jnp.tile(k_norm_g[0], 2 * N_KV_HEADS)
    gq = jnp.tile(q_norm_g[0], N_HEADS) * (HEAD_DIM ** -0.5 * LOG2_E)
    gn_lat = jnp.concatenate([gq, gk])[None, :]
    gn_ctx = gk[None, :]
    cos, sin = _rope_tables(n_tok)
    n1 = norm1_g[0][None, :]

    q, k_l, vt_l, u_l = _project(x, lat(sh1), lat(sc1), n1, wm_lat, w_vt, mavg, gn_lat, cos, sin)
    _, k_c, vt_c, u_c = _project(ctx, cx(sh1), cx(sc1), n1, wm_ctx, w_vt, mavg, gn_ctx, None, None)

    attn = _attention(q, k_l, k_c, vt_l, vt_c)

    a_re, a_im, bb_re, bb_im = _discretize(ssm_lambda_re[0], ssm_lambda_im[0], ssm_log_dt[0],
                                           ssm_b_re[0], ssm_b_im[0])
    y = _ssm_chunked(u_c, u_l, *_chunk_matrices(a_re, a_im, bb_re, bb_im, ssm_c_re[0], ssm_c_im[0]))

    mods = [lat(m) for m in (sh1, sc1, g1, sh2, sc2, g2)]
    return _merge(x, y, u_l, attn, mods, n1, norm2_g[0][None, :],
                  w_gate, ssm_d[0][None, :], w_glu[0].astype(BF16), b_glu[0][None, :],
                  w_br_attn[0].astype(BF16), w_br_ssm[0].astype(BF16), w_out[0].astype(BF16),
                  w_mlp1[0].astype(BF16), b_mlp1[0][None, :], w_mlp2[0].astype(BF16), b_mlp2[0][None, :])
```

```python
import functools

import jax
import jax.numpy as jnp
from jax import lax
from jax.experimental import pallas as pl
from jax.experimental.pallas import tpu as pltpu

F32 = jnp.float32
BF16 = jnp.bfloat16

D_MODEL = 1024
N_HEADS = 16
N_KV_HEADS = 4
HEAD_DIM = 64
KV_REP = N_HEADS // N_KV_HEADS
GRID_W = 64
ROPE_THETA = 10000.0
ROPE_AXIS_DIM = HEAD_DIM // 2
D_SSM = 512
SSM_GROUP = 16
N_SSM_GROUPS = 32
SSM_STATE = 64
N_STATE = N_SSM_GROUPS * SSM_STATE
D_FF = 4 * D_MODEL
NORM_EPS = 1e-6
OFF_K = 1024
OFF_V = 1280
OFF_U = 1536
OFF_GA = 2048

LANES = 128
SUBLANES = 8
MXU_TILE = 256
LOG2_E = 1.4426950408889634
VMEM_LIMIT_BYTES = 56 * 1024 * 1024

PROJ_ROWS = 2048
PROJ_SUB_ROWS = 256
ATTN_ROWS = 256
ATTN_KEYS = 256
ONES_ROWS = 16
MERGE_ROWS = 512
SSM_STEPS = 128
SSM_BATCH = SUBLANES
SCAN_LANES = 1024
SCAN_UNROLL = 8
FF_CHUNK = 1024


def _const_spec(shape):
    nd = len(shape)
    return pl.BlockSpec(shape, lambda *_: (0,) * nd, pipeline_mode=pl.Buffered(1))


def _params(n_axes):
    return pltpu.CompilerParams(dimension_semantics=("arbitrary",) * n_axes,
                                vmem_limit_bytes=VMEM_LIMIT_BYTES)


def _mod_body(c_ref, w_ref, b_ref, o_ref):
    c = c_ref[...]
    s = c * jax.nn.sigmoid(c)
    o_ref[...] = jnp.dot(s, w_ref[...], preferred_element_type=F32) + b_ref[...]


def _modulation(c_rows, w_mod, b_mod):
    rows = c_rows.shape[0]
    ncol = w_mod.shape[1]
    blk = D_MODEL
    return pl.pallas_call(
        _mod_body,
        grid=(ncol // blk,),
        in_specs=[pl.BlockSpec((rows, D_MODEL), lambda j: (0, 0)),
                  pl.BlockSpec((D_MODEL, blk), lambda j: (0, j)),
                  pl.BlockSpec((1, blk), lambda j: (0, j))],
        out_specs=pl.BlockSpec((rows, blk), lambda j: (0, j)),
        out_shape=jax.ShapeDtypeStruct((rows, ncol), F32),
        compiler_params=_params(1),
        name="mod",
    )(c_rows, w_mod, b_mod)


def _disc_body(lr_ref, li_ref, ldt_ref, br_ref, bi_ref, ar_ref, ai_ref, bbr_ref, bbi_ref):
    lr = lr_ref[...]
    li = li_ref[...]
    dt = jnp.exp(ldt_ref[...])
    mag = jnp.exp(lr * dt)
    ar = mag * jnp.cos(li * dt)
    ai = mag * jnp.sin(li * dt)
    nr = ar - 1.0
    den = lr * lr + li * li
    cr = (nr * lr + ai * li) / den
    ci = (ai * lr - nr * li) / den
    br = br_ref[...]
    bi = bi_ref[...]
    ar_ref[...] = ar
    ai_ref[...] = ai
    bbr_ref[...] = cr * br - ci * bi
    bbi_ref[...] = cr * bi + ci * br


def _discretize(lam_re, lam_im, log_dt, b_re, b_im):
    rows = 2 * N_SSM_GROUPS
    cols = SSM_STATE * SSM_GROUP
    rep = lambda a: jnp.repeat(a.reshape(rows, SSM_STATE), SSM_GROUP, axis=1)
    ldt = jnp.broadcast_to(log_dt.reshape(rows, 1), (rows, cols))
    spec = pl.BlockSpec((rows, cols), lambda: (0, 0))
    sds = jax.ShapeDtypeStruct((rows, cols), F32)
    ar, ai, bbr, bbi = pl.pallas_call(
        _disc_body,
        in_specs=[spec] * 5,
        out_specs=[spec] * 4,
        out_shape=[sds] * 4,
        name="disc",
    )(rep(lam_re), rep(lam_im), ldt, b_re.reshape(rows, cols), b_im.reshape(rows, cols))
    a_re = ar[:, ::SSM_GROUP].reshape(2, N_STATE)
    a_im = ai[:, ::SSM_GROUP].reshape(2, N_STATE)
    shape = (2, N_SSM_GROUPS, SSM_STATE, SSM_GROUP)
    return a_re, a_im, bbr.reshape(shape), bbi.reshape(shape)


def _ssm_matrices(a_re, a_im, bb_re, bb_im, c_re, c_im):
    eye = jnp.eye(16, dtype=F32)

    def in_blocks(bb):
        t = bb.reshape(2, 2, 16, SSM_STATE, SSM_GROUP).transpose(0, 1, 2, 4, 3)
        return jnp.einsum('dkghp,gG->dkghGp', t, eye).reshape(2, 2, 256, 1024)

    def out_blocks(c):
        t = c.reshape(2, 2, 16, SSM_GROUP, SSM_STATE).transpose(0, 1, 2, 4, 3)
        return jnp.einsum('dkgph,gG->dkgpGh', t, eye).reshape(2, 2, 1024, 256)

    bm = jnp.concatenate([in_blocks(bb_re), in_blocks(bb_im)], axis=-1).astype(BF16)
    cm = jnp.stack([out_blocks(c_re), out_blocks(-c_im)], axis=2).astype(BF16)
    bcast = lambda a: jnp.broadcast_to(a[:, None, :], (2, SUBLANES, N_STATE))
    return bm, cm, bcast(a_re), bcast(a_im)


def _rms_modulate(x, g, sc, sh):
    ms = jnp.mean(x * x, axis=-1, keepdims=True)
    xn = x * lax.rsqrt(ms + NORM_EPS) * g
    return xn * (1.0 + sc) + sh


def _proj_body(*refs, n_norm, rope):
    if rope:
        (x_ref, sh_ref, sc_ref, g_ref, wm_ref, wvt_ref, mavg_ref, gn_ref, cos_ref, sin_ref,
         q_ref, k_ref, vt_ref, u_ref) = refs
    else:
        (x_ref, sh_ref, sc_ref, g_ref, wm_ref, wvt_ref, mavg_ref, gn_ref,
         k_ref, vt_ref, u_ref) = refs
    rows = x_ref.shape[0]
    sub = min(rows, PROJ_SUB_ROWS)
    mavg = mavg_ref[...]
    width = mavg.shape[0]
    n_q_chunks = n_norm // LANES - k_ref.shape[0]
    if rope:
        lane = lax.broadcasted_iota(jnp.int32, (sub, LANES), 1)
        half = ROPE_AXIS_DIM // 2
        first_half = (lane % ROPE_AXIS_DIM) < half
    for r0 in range(0, rows, sub):
        rs = slice(r0, r0 + sub)
        h = _rms_modulate(x_ref[rs, :], g_ref[...], sc_ref[...], sh_ref[...])
        hb = h.astype(BF16)
        p = jnp.dot(hb, wm_ref[...], preferred_element_type=F32)
        qk = p[:, :n_norm]
        u_ref[rs, :] = p[:, n_norm:]
        sq = (qk * qk).astype(BF16)
        msq = jnp.concatenate(
            [jnp.dot(sq[:, width * j:width * (j + 1)], mavg, preferred_element_type=F32)
             for j in range(n_norm // width)], axis=1)
        vt = lax.dot_general(wvt_ref[...], hb, (((1,), (1,)), ((), ())), preferred_element_type=F32)
        for g in range(N_KV_HEADS):
            vt_ref[g, :, rs] = vt[HEAD_DIM * g:HEAD_DIM * (g + 1), :].astype(BF16)
        qkn = qk * lax.rsqrt(msq + NORM_EPS) * gn_ref[...]
        for c in range(n_norm // LANES):
            xc = qkn[:, LANES * c:LANES * (c + 1)]
            if rope:
                partner = jnp.where(first_half, pltpu.roll(xc, LANES - half, 1), pltpu.roll(xc, half, 1))
                xc = xc * cos_ref[rs, :] + partner * sin_ref[rs, :]
            if c < n_q_chunks:
                q_ref[c, rs, :] = xc.astype(BF16)
            else:
                k_ref[c - n_q_chunks, rs, :] = xc.astype(BF16)


def _project(x, sh, sc, g, wm, wvt, mavg, gn, cos, sin):
    bsz, n, _ = x.shape
    tm = min(PROJ_ROWS, n)
    rope = cos is not None
    n_norm = gn.shape[1]
    ncol = wm.shape[1]
    kv_w = N_KV_HEADS * HEAD_DIM
    q_chunks = D_MODEL // LANES
    mod_rows = sh.shape[0]
    mod_spec = pl.BlockSpec((None, 1, D_MODEL),
                            (lambda i, b: (b, 0, 0)) if mod_rows == bsz else (lambda i, b: (0, 0, 0)))
    in_specs = [pl.BlockSpec((None, tm, D_MODEL), lambda i, b: (b, i, 0)),
                mod_spec, mod_spec,
                _const_spec((1, D_MODEL)),
                _const_spec((D_MODEL, ncol)),
                _const_spec((kv_w, D_MODEL)),
                _const_spec(mavg.shape),
                _const_spec((1, n_norm))]
    args = [x, sh, sc, g, wm, wvt, mavg, gn]
    out_specs = [pl.BlockSpec((None, N_KV_HEADS, tm, LANES), lambda i, b: (b, 0, i, 0)),
                 pl.BlockSpec((None, N_KV_HEADS, HEAD_DIM, tm), lambda i, b: (b, 0, 0, i)),
                 pl.BlockSpec((None, tm, D_SSM), lambda i, b: (b, i, 0))]
    out_shape = [jax.ShapeDtypeStruct((bsz, N_KV_HEADS, n, LANES), BF16),
                 jax.ShapeDtypeStruct((bsz, N_KV_HEADS, HEAD_DIM, n), BF16),
                 jax.ShapeDtypeStruct((bsz, n, D_SSM), F32)]
    if rope:
        in_specs += [pl.BlockSpec((tm, LANES), lambda i, b: (i, 0))] * 2
        args += [cos, sin]
        out_specs = [pl.BlockSpec((None, q_chunks, tm, LANES), lambda i, b: (b, 0, i, 0))] + out_specs
        out_shape = [jax.ShapeDtypeStruct((bsz, q_chunks, n, LANES), BF16)] + out_shape
    outs = pl.pallas_call(
        functools.partial(_proj_body, n_norm=n_norm, rope=rope),
        grid=(n // tm, bsz),
        in_specs=in_specs,
        out_specs=out_specs,
        out_shape=out_shape,
        compiler_params=_params(2),
        name="proj_latent" if rope else "proj_ctx",
    )(*args)
    return outs if rope else [None] + list(outs)


def _attn_body(q_ref, qn_ref, kl_ref, kc_ref, vtl_ref, vtc_ref, o_ref, s0_scr, s1_scr, m_scr, o_scr):
    n_ctx = kc_ref.shape[1]
    n_lat = kl_ref.shape[1]
    key_blocks = ([(lo, min(ATTN_KEYS, n_ctx - lo)) for lo in range(0, n_ctx, ATTN_KEYS)]
                  + [(n_ctx + lo, ATTN_KEYS) for lo in range(0, n_lat, ATTN_KEYS)])
    tq = q_ref.shape[1]
    lane = lax.broadcasted_iota(jnp.int32, (tq, LANES), 1)
    head_lanes = (lane < HEAD_DIM, lane >= HEAD_DIM)
    s_bufs = (s0_scr, s1_scr)
    nt = (((1,), (1,)), ((), ()))
    reload_base = jnp.minimum(pl.program_id(1), 0)

    def slab_max(s):
        return jnp.max(s.reshape(s.shape[0] // SUBLANES, SUBLANES, tq), axis=0)

    def scores(qc, g, r, slot):
        s_scr = s_bufs[slot]
        qm = jnp.where(head_lanes[r % 2], qc, jnp.zeros_like(qc))
        sc = lax.dot_general(kc_ref[g], qm, nt, preferred_element_type=F32)
        sl = lax.dot_general(kl_ref[g], qm, nt, preferred_element_type=F32)
        s_scr[:n_ctx, :] = sc
        s_scr[n_ctx:, :] = sl
        m_scr[slot] = jnp.maximum(slab_max(sc), slab_max(sl))

    def finish(g, r, slot):
        s_scr = s_bufs[slot]
        m = jnp.max(m_scr[slot], axis=0, keepdims=True)
        m8 = jnp.broadcast_to(m, (SUBLANES, tq))
        acc = jnp.zeros((HEAD_DIM + ONES_ROWS, tq), F32)
        for lo, size in key_blocks:
            row = pl.multiple_of(reload_base + lo, SUBLANES)
            s = s_scr[pl.ds(row, size), :].reshape(size // SUBLANES, SUBLANES, tq)
            pb = jnp.exp2(s - m8).reshape(size, tq).astype(BF16)
            if lo < n_ctx:
                vt = vtc_ref[g, :, lo:lo + size]
            else:
                vt = vtl_ref[g, :, lo - n_ctx:lo - n_ctx + size]
            ones_rows = jnp.ones((ONES_ROWS, size), BF16)
            acc = acc + jnp.dot(jnp.concatenate([vt, ones_rows], axis=0), pb, preferred_element_type=F32)
        o_scr[KV_REP * g + r] = acc

    @pl.when(pl.program_id(1) == 0)
    def _():
        scores(q_ref[0], 0, 0, 0)

    def group(g, carry):
        for r in range(KV_REP):
            if r + 1 < KV_REP:
                scores(q_ref[2 * g + (r + 1) // 2], g, r + 1, (r + 1) % 2)
            else:
                last = g == N_KV_HEADS - 1
                qc = jnp.where(last, qn_ref[0], q_ref[2 * jnp.minimum(g + 1, N_KV_HEADS - 1)])
                scores(qc, jnp.where(last, 0, g + 1), 0, (r + 1) % 2)
            finish(g, r, r % 2)
        return carry

    lax.fori_loop(0, N_KV_HEADS, group, 0)
    o_all = o_scr[...]
    o_all = o_all[:, :HEAD_DIM, :] / o_all[:, HEAD_DIM:HEAD_DIM + 1, :]
    o_ref[...] = o_all.reshape(N_HEADS * HEAD_DIM, tq).T.astype(BF16)


def _attention(q, k_l, k_c, vt_l, vt_c):
    bsz, q_chunks, n, _ = q.shape
    n_ctx = k_c.shape[2]
    tq = ATTN_ROWS
    whole = lambda shape: pl.BlockSpec((None,) + shape, lambda b, j: (b,) + (0,) * len(shape))
    return pl.pallas_call(
        _attn_body,
        grid=(bsz, n // tq),
        in_specs=[pl.BlockSpec((None, q_chunks, tq, LANES), lambda b, j: (b, 0, j, 0)),
                  pl.BlockSpec((None, 1, tq, LANES), lambda b, j: (b, 0, jnp.minimum(j + 1, n // tq - 1), 0)),
                  whole((N_KV_HEADS, n, LANES)),
                  whole((N_KV_HEADS, n_ctx, LANES)),
                  whole((N_KV_HEADS, HEAD_DIM, n)),
                  whole((N_KV_HEADS, HEAD_DIM, n_ctx))],
        out_specs=pl.BlockSpec((None, tq, D_MODEL), lambda b, j: (b, j, 0)),
        out_shape=jax.ShapeDtypeStruct((bsz, n, D_MODEL), BF16),
        scratch_shapes=[pltpu.VMEM((n_ctx + n, tq), F32),
                        pltpu.VMEM((n_ctx + n, tq), F32),
                        pltpu.VMEM((2, SUBLANES, tq), F32),
                        pltpu.VMEM((N_HEADS, HEAD_DIM + ONES_ROWS, tq), F32)],
        compiler_params=_params(2),
        name="attn",
    )(q, q, k_l, k_c, vt_l, vt_c)


def _ssm_body(uc_ref, ul_ref, bm_ref, cm_ref, ar_ref, ai_ref, y_ref,
              ut_scr, yt_scr, st_scr, hr_scr, hi_scr, *, n_ctx_tiles):
    d = pl.program_id(0)
    i = pl.program_id(2)
    steps = uc_ref.shape[1]
    half_w = N_STATE // 2
    n_slabs = D_SSM // LANES

    @pl.when(i == 0)
    def _():
        hr_scr[...] = jnp.zeros_like(hr_scr)
        hi_scr[...] = jnp.zeros_like(hi_scr)

    u = jnp.where(i < n_ctx_tiles, uc_ref[...], ul_ref[...])
    for b in range(SSM_BATCH):
        for c in range(n_slabs):
            ut_scr[c, pl.ds(b, steps, stride=SSM_BATCH), :] = u[b, :, LANES * c:LANES * (c + 1)]
    ub = jnp.concatenate([ut_scr[c] for c in range(n_slabs)], axis=1).astype(BF16)
    for k in range(2):
        r = jnp.dot(ub[:, 256 * k:256 * (k + 1)], bm_ref[k], preferred_element_type=F32)
        st_scr[:, half_w * k:half_w * (k + 1)] = r[:, :half_w]
        st_scr[:, N_STATE + half_w * k:N_STATE + half_w * (k + 1)] = r[:, half_w:]

    def scan(forward):
        for c in range(N_STATE // SCAN_LANES):
            re = slice(SCAN_LANES * c, SCAN_LANES * (c + 1))
            im = slice(N_STATE + SCAN_LANES * c, N_STATE + SCAN_LANES * (c + 1))
            a_r = ar_ref[:, re]
            a_i = ai_ref[:, re]

            def step(j, carry):
                hr, hi = carry
                t = j if forward else steps - 1 - j
                row = pl.multiple_of(t * SSM_BATCH, SSM_BATCH)
                br = st_scr[pl.ds(row, SSM_BATCH), re]
                bi = st_scr[pl.ds(row, SSM_BATCH), im]
                hr, hi = a_r * hr - a_i * hi + br, a_r * hi + a_i * hr + bi
                st_scr[pl.ds(row, SSM_BATCH), re] = hr
                st_scr[pl.ds(row, SSM_BATCH), im] = hi
                return hr, hi

            hr, hi = lax.fori_loop(0, steps, step, (hr_scr[:, re], hi_scr[:, re]), unroll=SCAN_UNROLL)
            hr_scr[:, re] = hr
            hi_scr[:, re] = hi

    @pl.when(d == 0)
    def _():
        scan(True)

    @pl.when(d == 1)
    def _():
        scan(False)

    @pl.when(i >= n_ctx_tiles)
    def _():
        for k in range(2):
            y = (jnp.dot(st_scr[:, half_w * k:half_w * (k + 1)].astype(BF16), cm_ref[k, 0],
                         preferred_element_type=F32)
                 + jnp.dot(st_scr[:, N_STATE + half_w * k:N_STATE + half_w * (k + 1)].astype(BF16), cm_ref[k, 1],
                           preferred_element_type=F32))
            for c in range(2):
                yt_scr[2 * k + c] = y[:, LANES * c:LANES * (c + 1)]
        for b in range(SSM_BATCH):
            y_ref[b] = jnp.concatenate(
                [yt_scr[c, pl.ds(b, steps, stride=SSM_BATCH), :] for c in range(n_slabs)], axis=1)


def _ssm(u_c, u_l, bm, cm, a_re, a_im):
    bsz, n_ctx, _ = u_c.shape
    n = u_l.shape[1]
    t = SSM_STEPS
    nct = n_ctx // t
    nlt = n // t
    rows = t * SSM_BATCH

    def ctx_tile(d, i):
        return jnp.where(d == 0, jnp.minimum(i, nct - 1), jnp.maximum(nct - 1 - i, 0))

    def lat_tile(d, i):
        j = jnp.maximum(i - nct, 0)
        return jnp.where(d == 0, j, nlt - 1 - j)

    once = pl.Buffered(1)
    return pl.pallas_call(
        functools.partial(_ssm_body, n_ctx_tiles=nct),
        grid=(2, bsz // SSM_BATCH, nct + nlt),
        in_specs=[pl.BlockSpec((SSM_BATCH, t, D_SSM), lambda d, bh, i: (bh, ctx_tile(d, i), 0)),
                  pl.BlockSpec((SSM_BATCH, t, D_SSM), lambda d, bh, i: (bh, lat_tile(d, i), 0)),
                  pl.BlockSpec((None, 2, 256, N_STATE), lambda d, bh, i: (d, 0, 0, 0), pipeline_mode=once),
                  pl.BlockSpec((None, 2, 2, N_STATE // 2, 256), lambda d, bh, i: (d, 0, 0, 0, 0),
                               pipeline_mode=once),
                  pl.BlockSpec((None, SUBLANES, N_STATE), lambda d, bh, i: (d, 0, 0), pipeline_mode=once),
                  pl.BlockSpec((None, SUBLANES, N_STATE), lambda d, bh, i: (d, 0, 0), pipeline_mode=once)],
        out_specs=pl.BlockSpec((None, SSM_BATCH, t, D_SSM), lambda d, bh, i: (d, bh, lat_tile(d, i), 0)),
        out_shape=jax.ShapeDtypeStruct((2, bsz, n, D_SSM), F32),
        scratch_shapes=[pltpu.VMEM((D_SSM // LANES, rows, LANES), F32),
                        pltpu.VMEM((D_SSM // LANES, rows, LANES), F32),
                        pltpu.VMEM((rows, 2 * N_STATE), F32),
                        pltpu.VMEM((SSM_BATCH, N_STATE), F32),
                        pltpu.VMEM((SSM_BATCH, N_STATE), F32)],
        compiler_params=_params(3),
        name="ssm",
    )(u_c, u_l, bm, cm, a_re, a_im)


def _merge_body(x_ref, yf_ref, yb_ref, u_ref, at_ref,
                sh1_ref, sc1_ref, g1_ref, sh2_ref, sc2_ref, g2_ref, n1_ref, n2_ref,
                wg_ref, dv_ref, wglu_ref, bglu_ref, wba_ref, wbs_ref, wo_ref,
                w1_ref, b1_ref, w2_ref, b2_ref, o_ref):
    x = x_ref[...]
    hb = _rms_modulate(x, n1_ref[...], sc1_ref[...], sh1_ref[...]).astype(BF16)
    gates = jnp.dot(hb, wg_ref[...], preferred_element_type=F32)
    gate_a = jax.nn.sigmoid(gates[:, :D_MODEL])
    gate_s = jax.nn.sigmoid(gates[:, D_MODEL:])

    y = yf_ref[...] + yb_ref[...] + dv_ref[...] * u_ref[...]
    y = jax.nn.gelu(y)
    z = jnp.dot(y.astype(BF16), wglu_ref[...], preferred_element_type=F32) + bglu_ref[...]
    y = y * jax.nn.sigmoid(z)

    merged = (gate_a * jnp.dot(at_ref[...], wba_ref[...], preferred_element_type=F32)
              + gate_s * jnp.dot(y.astype(BF16), wbs_ref[...], preferred_element_type=F32))
    mix = jnp.dot(merged.astype(BF16), wo_ref[...], preferred_element_type=F32)
    x1 = x + g1_ref[...] * mix

    h2 = _rms_modulate(x1, n2_ref[...], sc2_ref[...], sh2_ref[...]).astype(BF16)
    acc = jnp.zeros_like(x1)
    for c in range(D_FF // FF_CHUNK):
        cols = slice(FF_CHUNK * c, FF_CHUNK * (c + 1))
        hid = jnp.dot(h2, w1_ref[:, cols], preferred_element_type=F32) + b1_ref[:, cols]
        hid = jnp.square(jnp.maximum(hid, 0.0))
        acc = acc + jnp.dot(hid.astype(BF16), w2_ref[cols, :], preferred_element_type=F32)
    o_ref[...] = x1 + g2_ref[...] * (acc + b2_ref[...])


def _merge(x, y2, u2, attn, mods, n1, n2, wg, dvec, wglu, bglu, wba, wbs, wo, w1, b1, w2, b2):
    bsz, n, _ = x.shape
    tm = MERGE_ROWS
    row_spec = pl.BlockSpec((None, tm, D_MODEL), lambda b, i: (b, i, 0))
    mod_spec = pl.BlockSpec((None, 1, D_MODEL), lambda b, i: (b, 0, 0))
    in_specs = [row_spec,
                pl.BlockSpec((None, None, tm, D_SSM), lambda b, i: (0, b, i, 0)),
                pl.BlockSpec((None, None, tm, D_SSM), lambda b, i: (1, b, i, 0)),
                pl.BlockSpec((None, tm, D_SSM), lambda b, i: (b, i, 0)),
                row_spec] + [mod_spec] * 6
    consts = [n1, n2, wg, dvec, wglu, bglu, wba, wbs, wo, w1, b1, w2, b2]
    in_specs += [_const_spec(a.shape) for a in consts]
    return pl.pallas_call(
        _merge_body,
        grid=(bsz, n // tm),
        in_specs=in_specs,
        out_specs=row_spec,
        out_shape=jax.ShapeDtypeStruct(x.shape, x.dtype),
        compiler_params=_params(2),
        name="merge",
    )(x, y2, y2, u2, attn, *mods, *consts)


def _rope_tables(n_tok):
    half = ROPE_AXIS_DIM // 2
    pos = jnp.arange(n_tok, dtype=jnp.int32)
    row = (pos // GRID_W).astype(F32)
    col = (pos % GRID_W).astype(F32)
    inv_freq = ROPE_THETA ** (-jnp.arange(half, dtype=F32) / half)
    ang_r = row[:, None] * inv_freq
    ang_c = col[:, None] * inv_freq
    cos = jnp.concatenate([jnp.cos(ang_r)] * 2 + [jnp.cos(ang_c)] * 2, axis=-1)
    sin = jnp.concatenate([-jnp.sin(ang_r), jnp.sin(ang_r), -jnp.sin(ang_c), jnp.sin(ang_c)], axis=-1)
    reps = LANES // HEAD_DIM
    return jnp.tile(cos, (1, reps)), jnp.tile(sin, (1, reps))


def kernel(x, c, ctx, c_ctx, w_mod, b_mod, norm1_g, norm2_g, w_in, q_norm_g, k_norm_g,
           ssm_lambda_re, ssm_lambda_im, ssm_log_dt, ssm_b_re, ssm_b_im, ssm_c_re, ssm_c_im,
           ssm_d, w_glu, b_glu, w_br_attn, w_br_ssm, w_out, w_mlp1, b_mlp1, w_mlp2, b_mlp2):
    bsz, n_tok, d_model = x.shape
    n_ctx = ctx.shape[1]
    assert d_model == D_MODEL and w_mod.shape[0] == 1, "single-layer block of width 1024 only"
    assert n_tok % PROJ_ROWS == 0 and n_ctx % min(PROJ_ROWS, n_ctx) == 0 and n_ctx % SSM_STEPS == 0
    assert bsz % SSM_BATCH == 0 and n_tok % GRID_W == 0

    pad = (-(bsz + 1)) % SUBLANES
    c_rows = jnp.concatenate([c, c_ctx[None, :], jnp.zeros((pad, D_MODEL), F32)], axis=0)
    mod = _modulation(c_rows, w_mod[0], b_mod[0][None, :])
    sh1, sc1, g1, sh2, sc2, g2 = [m[:, None, :] for m in jnp.split(mod, 6, axis=-1)]
    lat = lambda m: m[:bsz]
    cx = lambda m: m[bsz:bsz + 1]

    w = w_in[0]
    w_k = w[:, OFF_K:OFF_V].reshape(D_MODEL, N_KV_HEADS, 1, HEAD_DIM)
    w_kdup = jnp.broadcast_to(w_k, (D_MODEL, N_KV_HEADS, 2, HEAD_DIM)).reshape(D_MODEL, 2 * OFF_V - 2 * OFF_K)
    w_u = w[:, OFF_U:OFF_GA]
    wm_lat = jnp.concatenate([w[:, :OFF_K], w_kdup, w_u], axis=1).astype(BF16)
    wm_ctx = jnp.concatenate([w_kdup, w_u], axis=1).astype(BF16)
    w_vt = w[:, OFF_V:OFF_U].T.astype(BF16)
    w_gate = w[:, OFF_GA:].astype(BF16)
    mavg = jnp.kron(jnp.eye(MXU_TILE // HEAD_DIM, dtype=F32),
                    jnp.full((HEAD_DIM, HEAD_DIM), 1.0 / HEAD_DIM, F32)).astype(BF16)
    gk = jnp.tile(k_norm_g[0], 2 * N_KV_HEADS)
    gq = jnp.tile(q_norm_g[0], N_HEADS) * (HEAD_DIM ** -0.5 * LOG2_E)
    gn_lat = jnp.concatenate([gq, gk])[None, :]
    gn_ctx = gk[None, :]
    cos, sin = _rope_tables(n_tok)
    n1 = norm1_g[0][None, :]

    q, k_l, vt_l, u_l = _project(x, lat(sh1), lat(sc1), n1, wm_lat, w_vt, mavg, gn_lat, cos, sin)
    _, k_c, vt_c, u_c = _project(ctx, cx(sh1), cx(sc1), n1, wm_ctx, w_vt, mavg, gn_ctx, None, None)

    attn = _attention(q, k_l, k_c, vt_l, vt_c)

    a_re, a_im, bb_re, bb_im = _discretize(ssm_lambda_re[0], ssm_lambda_im[0], ssm_log_dt[0],
                                           ssm_b_re[0], ssm_b_im[0])
    bm, cm, a_re8, a_im8 = _ssm_matrices(a_re, a_im, bb_re, bb_im, ssm_c_re[0], ssm_c_im[0])
    y = _ssm(u_c, u_l, bm, cm, a_re8, a_im8)

    mods = [lat(m) for m in (sh1, sc1, g1, sh2, sc2, g2)]
    return _merge(x, y, u_l, attn, mods, n1, norm2_g[0][None, :],
                  w_gate, ssm_d[0][None, :], w_glu[0].astype(BF16), b_glu[0][None, :],
                  w_br_attn[0].astype(BF16), w_br_ssm[0].astype(BF16), w_out[0].astype(BF16),
                  w_mlp1[0].astype(BF16), b_mlp1[0][None, :], w_mlp2[0].astype(BF16), b_mlp2[0][None, :])
```

```python
import functools

import jax
import jax.numpy as jnp
from jax import lax
from jax.experimental import pallas as pl
from jax.experimental.pallas import tpu as pltpu

F32 = jnp.float32
BF16 = jnp.bfloat16

D_MODEL = 1024
N_HEADS = 16
N_KV_HEADS = 4
HEAD_DIM = 64
KV_REP = N_HEADS // N_KV_HEADS
GRID_W = 64
ROPE_THETA = 10000.0
ROPE_AXIS_DIM = HEAD_DIM // 2
D_SSM = 512
SSM_GROUP = 16
N_SSM_GROUPS = 32
SSM_STATE = 64
N_STATE = N_SSM_GROUPS * SSM_STATE
D_FF = 4 * D_MODEL
NORM_EPS = 1e-6
OFF_K = 1024
OFF_V = 1280
OFF_U = 1536
OFF_GA = 2048

LANES = 128
SUBLANES = 8
MXU_TILE = 256
LOG2_E = 1.4426950408889634
VMEM_LIMIT_BYTES = 56 * 1024 * 1024

PROJ_ROWS = 2048
PROJ_SUB_ROWS = 256
ATTN_ROWS = 256
ATTN_KEYS = 256
ONES_ROWS = 16
MERGE_ROWS = 512
SSM_STEPS = 128
SSM_BATCH = SUBLANES
SCAN_LANES = 1024
SCAN_UNROLL = 8
FF_CHUNK = 1024


def _const_spec(shape):
    nd = len(shape)
    return pl.BlockSpec(shape, lambda *_: (0,) * nd, pipeline_mode=pl.Buffered(1))


def _params(n_axes):
    return pltpu.CompilerParams(dimension_semantics=("arbitrary",) * n_axes,
                                vmem_limit_bytes=VMEM_LIMIT_BYTES)


def _mod_body(c_ref, w_ref, b_ref, o_ref):
    c = c_ref[...]
    s = c * jax.nn.sigmoid(c)
    o_ref[...] = jnp.dot(s, w_ref[...], preferred_element_type=F32) + b_ref[...]


def _modulation(c_rows, w_mod, b_mod):
    rows = c_rows.shape[0]
    ncol = w_mod.shape[1]
    blk = D_MODEL
    return pl.pallas_call(
        _mod_body,
        grid=(ncol // blk,),
        in_specs=[pl.BlockSpec((rows, D_MODEL), lambda j: (0, 0)),
                  pl.BlockSpec((D_MODEL, blk), lambda j: (0, j)),
                  pl.BlockSpec((1, blk), lambda j: (0, j))],
        out_specs=pl.BlockSpec((rows, blk), lambda j: (0, j)),
        out_shape=jax.ShapeDtypeStruct((rows, ncol), F32),
        compiler_params=_params(1),
        name="mod",
    )(c_rows, w_mod, b_mod)


def _disc_body(lr_ref, li_ref, ldt_ref, br_ref, bi_ref, ar_ref, ai_ref, bbr_ref, bbi_ref):
    lr = lr_ref[...]
    li = li_ref[...]
    dt = jnp.exp(ldt_ref[...])
    mag = jnp.exp(lr * dt)
    ar = mag * jnp.cos(li * dt)
    ai = mag * jnp.sin(li * dt)
    nr = ar - 1.0
    den = lr * lr + li * li
    cr = (nr * lr + ai * li) / den
    ci = (ai * lr - nr * li) / den
    br = br_ref[...]
    bi = bi_ref[...]
    ar_ref[...] = ar
    ai_ref[...] = ai
    bbr_ref[...] = cr * br - ci * bi
    bbi_ref[...] = cr * bi + ci * br


def _discretize(lam_re, lam_im, log_dt, b_re, b_im):
    rows = 2 * N_SSM_GROUPS
    cols = SSM_STATE * SSM_GROUP
    rep = lambda a: jnp.repeat(a.reshape(rows, SSM_STATE), SSM_GROUP, axis=1)
    ldt = jnp.broadcast_to(log_dt.reshape(rows, 1), (rows, cols))
    spec = pl.BlockSpec((rows, cols), lambda: (0, 0))
    sds = jax.ShapeDtypeStruct((rows, cols), F32)
    ar, ai, bbr, bbi = pl.pallas_call(
        _disc_body,
        in_specs=[spec] * 5,
        out_specs=[spec] * 4,
        out_shape=[sds] * 4,
        name="disc",
    )(rep(lam_re), rep(lam_im), ldt, b_re.reshape(rows, cols), b_im.reshape(rows, cols))
    a_re = ar[:, ::SSM_GROUP].reshape(2, N_STATE)
    a_im = ai[:, ::SSM_GROUP].reshape(2, N_STATE)
    shape = (2, N_SSM_GROUPS, SSM_STATE, SSM_GROUP)
    return a_re, a_im, bbr.reshape(shape), bbi.reshape(shape)


def _ssm_matrices(a_re, a_im, bb_re, bb_im, c_re, c_im):
    eye = jnp.eye(16, dtype=F32)

    def in_blocks(bb):
        t = bb.reshape(2, 2, 16, SSM_STATE, SSM_GROUP).transpose(0, 1, 2, 4, 3)
        return jnp.einsum('dkghp,gG->dkghGp', t, eye).reshape(2, 2, 256, 1024)

    def out_blocks(c):
        t = c.reshape(2, 2, 16, SSM_GROUP, SSM_STATE).transpose(0, 1, 2, 4, 3)
        return jnp.einsum('dkgph,gG->dkgpGh', t, eye).reshape(2, 2, 1024, 256)

    bm = jnp.concatenate([in_blocks(bb_re), in_blocks(bb_im)], axis=-1).astype(BF16)
    cm = jnp.stack([out_blocks(c_re), out_blocks(-c_im)], axis=2).astype(BF16)
    bcast = lambda a: jnp.broadcast_to(a[:, None, :], (2, SUBLANES, N_STATE))
    return bm, cm, bcast(a_re), bcast(a_im)


def _rms_modulate(x, g, sc, sh):
    ms = jnp.mean(x * x, axis=-1, keepdims=True)
    xn = x * lax.rsqrt(ms + NORM_EPS) * g
    return xn * (1.0 + sc) + sh


def _proj_body(*refs, n_norm, rope):
    if rope:
        (x_ref, sh_ref, sc_ref, g_ref, wm_ref, wvt_ref, mavg_ref, gn_ref, cos_ref, sin_ref,
         q_ref, k_ref, vt_ref, u_ref) = refs
    else:
        (x_ref, sh_ref, sc_ref, g_ref, wm_ref, wvt_ref, mavg_ref, gn_ref,
         k_ref, vt_ref, u_ref) = refs
    rows = x_ref.shape[0]
    sub = min(rows, PROJ_SUB_ROWS)
    mavg = mavg_ref[...]
    width = mavg.shape[0]
    n_q_chunks = n_norm // LANES - k_ref.shape[0]
    if rope:
        lane = lax.broadcasted_iota(jnp.int32, (sub, LANES), 1)
        half = ROPE_AXIS_DIM // 2
        first_half = (lane % ROPE_AXIS_DIM) < half
    for r0 in range(0, rows, sub):
        rs = slice(r0, r0 + sub)
        h = _rms_modulate(x_ref[rs, :], g_ref[...], sc_ref[...], sh_ref[...])
        hb = h.astype(BF16)
        p = jnp.dot(hb, wm_ref[...], preferred_element_type=F32)
        qk = p[:, :n_norm]
        u_ref[rs, :] = p[:, n_norm:]
        sq = (qk * qk).astype(BF16)
        msq = jnp.concatenate(
            [jnp.dot(sq[:, width * j:width * (j + 1)], mavg, preferred_element_type=F32)
             for j in range(n_norm // width)], axis=1)
        vt = lax.dot_general(wvt_ref[...], hb, (((1,), (1,)), ((), ())), preferred_element_type=F32)
        for g in range(N_KV_HEADS):
            vt_ref[g, :, rs] = vt[HEAD_DIM * g:HEAD_DIM * (g + 1), :].astype(BF16)
        qkn = qk * lax.rsqrt(msq + NORM_EPS) * gn_ref[...]
        for c in range(n_norm // LANES):
            xc = qkn[:, LANES * c:LANES * (c + 1)]
            if rope:
                partner = jnp.where(first_half, pltpu.roll(xc, LANES - half, 1), pltpu.roll(xc, half, 1))
                xc = xc * cos_ref[rs, :] + partner * sin_ref[rs, :]
            if c < n_q_chunks:
                q_ref[c, rs, :] = xc.astype(BF16)
            else:
                k_ref[c - n_q_chunks, rs, :] = xc.astype(BF16)


def _project(x, mod, mod_row, g, wm, wvt, mavg, gn, cos, sin):
    bsz, n, _ = x.shape
    tm = min(PROJ_ROWS, n)
    rope = cos is not None
    n_norm = gn.shape[1]
    ncol = wm.shape[1]
    kv_w = N_KV_HEADS * HEAD_DIM
    q_chunks = D_MODEL // LANES
    row = (lambda b: b) if mod_row is None else (lambda b: mod_row)
    in_specs = [pl.BlockSpec((None, tm, D_MODEL), lambda i, b: (b, i, 0)),
                pl.BlockSpec((None, None, 1, D_MODEL), lambda i, b: (row(b), 0, 0, 0)),
                pl.BlockSpec((None, None, 1, D_MODEL), lambda i, b: (row(b), 1, 0, 0)),
                _const_spec((1, D_MODEL)),
                _const_spec((D_MODEL, ncol)),
                _const_spec((kv_w, D_MODEL)),
                _const_spec(mavg.shape),
                _const_spec((1, n_norm))]
    args = [x, mod, mod, g, wm, wvt, mavg, gn]
    out_specs = [pl.BlockSpec((None, N_KV_HEADS, tm, LANES), lambda i, b: (b, 0, i, 0)),
                 pl.BlockSpec((None, N_KV_HEADS, HEAD_DIM, tm), lambda i, b: (b, 0, 0, i)),
                 pl.BlockSpec((None, tm, D_SSM), lambda i, b: (b, i, 0))]
    out_shape = [jax.ShapeDtypeStruct((bsz, N_KV_HEADS, n, LANES), BF16),
                 jax.ShapeDtypeStruct((bsz, N_KV_HEADS, HEAD_DIM, n), BF16),
                 jax.ShapeDtypeStruct((bsz, n, D_SSM), F32)]
    if rope:
        in_specs += [pl.BlockSpec((tm, LANES), lambda i, b: (i, 0))] * 2
        args += [cos, sin]
        out_specs = [pl.BlockSpec((None, q_chunks, tm, LANES), lambda i, b: (b, 0, i, 0))] + out_specs
        out_shape = [jax.ShapeDtypeStruct((bsz, q_chunks, n, LANES), BF16)] + out_shape
    outs = pl.pallas_call(
        functools.partial(_proj_body, n_norm=n_norm, rope=rope),
        grid=(n // tm, bsz),
        in_specs=in_specs,
        out_specs=out_specs,
        out_shape=out_shape,
        compiler_params=_params(2),
        name="proj_latent" if rope else "proj_ctx",
    )(*args)
    return outs if rope else [None] + list(outs)


def _attn_body(q_ref, qn_ref, kl_ref, kc_ref, vtl_ref, vtc_ref, o_ref, s0_scr, s1_scr, m_scr, o_scr):
    n_ctx = kc_ref.shape[1]
    n_lat = kl_ref.shape[1]
    key_blocks = ([(lo, min(ATTN_KEYS, n_ctx - lo)) for lo in range(0, n_ctx, ATTN_KEYS)]
                  + [(n_ctx + lo, ATTN_KEYS) for lo in range(0, n_lat, ATTN_KEYS)])
    tq = q_ref.shape[1]
    lane = lax.broadcasted_iota(jnp.int32, (tq, LANES), 1)
    head_lanes = (lane < HEAD_DIM, lane >= HEAD_DIM)
    s_bufs = (s0_scr, s1_scr)
    nt = (((1,), (1,)), ((), ()))
    reload_base = jnp.minimum(pl.program_id(1), 0)

    def slab_max(s):
        return jnp.max(s.reshape(s.shape[0] // SUBLANES, SUBLANES, tq), axis=0)

    def scores(qc, g, r, slot):
        s_scr = s_bufs[slot]
        qm = jnp.where(head_lanes[r % 2], qc, jnp.zeros_like(qc))
        sc = lax.dot_general(kc_ref[g], qm, nt, preferred_element_type=F32)
        sl = lax.dot_general(kl_ref[g], qm, nt, preferred_element_type=F32)
        s_scr[:n_ctx, :] = sc
        s_scr[n_ctx:, :] = sl
        m_scr[slot] = jnp.maximum(slab_max(sc), slab_max(sl))

    def finish(g, r, slot):
        s_scr = s_bufs[slot]
        m = jnp.max(m_scr[slot], axis=0, keepdims=True)
        m8 = jnp.broadcast_to(m, (SUBLANES, tq))
        acc = jnp.zeros((HEAD_DIM + ONES_ROWS, tq), F32)
        for lo, size in key_blocks:
            row = pl.multiple_of(reload_base + lo, SUBLANES)
            s = s_scr[pl.ds(row, size), :].reshape(size // SUBLANES, SUBLANES, tq)
            pb = jnp.exp2(s - m8).reshape(size, tq).astype(BF16)
            if lo < n_ctx:
                vt = vtc_ref[g, :, lo:lo + size]
            else:
                vt = vtl_ref[g, :, lo - n_ctx:lo - n_ctx + size]
            ones_rows = jnp.ones((ONES_ROWS, size), BF16)
            acc = acc + jnp.dot(jnp.concatenate([vt, ones_rows], axis=0), pb, preferred_element_type=F32)
        o_scr[KV_REP * g + r] = acc

    @pl.when(pl.program_id(1) == 0)
    def _():
        scores(q_ref[0], 0, 0, 0)

    def group(g, carry):
        for r in range(KV_REP):
            if r + 1 < KV_REP:
                scores(q_ref[2 * g + (r + 1) // 2], g, r + 1, (r + 1) % 2)
            else:
                last = g == N_KV_HEADS - 1
                qc = jnp.where(last, qn_ref[0], q_ref[2 * jnp.minimum(g + 1, N_KV_HEADS - 1)])
                scores(qc, jnp.where(last, 0, g + 1), 0, (r + 1) % 2)
            finish(g, r, r % 2)
        return carry

    lax.fori_loop(0, N_KV_HEADS, group, 0)
    o_all = o_scr[...]
    o_all = o_all[:, :HEAD_DIM, :] / o_all[:, HEAD_DIM:HEAD_DIM + 1, :]
    o_ref[...] = o_all.reshape(N_HEADS * HEAD_DIM, tq).T.astype(BF16)


def _attention(q, k_l, k_c, vt_l, vt_c):
    bsz, q_chunks, n, _ = q.shape
    n_ctx = k_c.shape[2]
    tq = ATTN_ROWS
    whole = lambda shape: pl.BlockSpec((None,) + shape, lambda b, j: (b,) + (0,) * len(shape))
    return pl.pallas_call(
        _attn_body,
        grid=(bsz, n // tq),
        in_specs=[pl.BlockSpec((None, q_chunks, tq, LANES), lambda b, j: (b, 0, j, 0)),
                  pl.BlockSpec((None, 1, tq, LANES), lambda b, j: (b, 0, jnp.minimum(j + 1, n // tq - 1), 0)),
                  whole((N_KV_HEADS, n, LANES)),
                  whole((N_KV_HEADS, n_ctx, LANES)),
                  whole((N_KV_HEADS, HEAD_DIM, n)),
                  whole((N_KV_HEADS, HEAD_DIM, n_ctx))],
        out_specs=pl.BlockSpec((None, tq, D_MODEL), lambda b, j: (b, j, 0)),
        out_shape=jax.ShapeDtypeStruct((bsz, n, D_MODEL), BF16),
        scratch_shapes=[pltpu.VMEM((n_ctx + n, tq), F32),
                        pltpu.VMEM((n_ctx + n, tq), F32),
                        pltpu.VMEM((2, SUBLANES, tq), F32),
                        pltpu.VMEM((N_HEADS, HEAD_DIM + ONES_ROWS, tq), F32)],
        compiler_params=_params(2),
        name="attn",
    )(q, q, k_l, k_c, vt_l, vt_c)


def _ssm_body(uc_ref, ul_ref, bm_ref, cm_ref, ar_ref, ai_ref, y_ref,
              ut_scr, yt_scr, st_scr, hr_scr, hi_scr, *, n_ctx_tiles):
    d = pl.program_id(0)
    i = pl.program_id(2)
    steps = uc_ref.shape[1]
    half_w = N_STATE // 2
    n_slabs = D_SSM // LANES

    @pl.when(i == 0)
    def _():
        hr_scr[...] = jnp.zeros_like(hr_scr)
        hi_scr[...] = jnp.zeros_like(hi_scr)

    u = jnp.where(i < n_ctx_tiles, uc_ref[...], ul_ref[...])
    for b in range(SSM_BATCH):
        for c in range(n_slabs):
            ut_scr[c, pl.ds(b, steps, stride=SSM_BATCH), :] = u[b, :, LANES * c:LANES * (c + 1)]
    ub = jnp.concatenate([ut_scr[c] for c in range(n_slabs)], axis=1).astype(BF16)
    for k in range(2):
        r = jnp.dot(ub[:, 256 * k:256 * (k + 1)], bm_ref[k], preferred_element_type=F32)
        st_scr[:, half_w * k:half_w * (k + 1)] = r[:, :half_w]
        st_scr[:, N_STATE + half_w * k:N_STATE + half_w * (k + 1)] = r[:, half_w:]

    def scan(forward):
        for c in range(N_STATE // SCAN_LANES):
            re = slice(SCAN_LANES * c, SCAN_LANES * (c + 1))
            im = slice(N_STATE + SCAN_LANES * c, N_STATE + SCAN_LANES * (c + 1))
            a_r = ar_ref[:, re]
            a_i = ai_ref[:, re]

            def step(j, carry):
                hr, hi = carry
                t = j if forward else steps - 1 - j
                row = pl.multiple_of(t * SSM_BATCH, SSM_BATCH)
                br = st_scr[pl.ds(row, SSM_BATCH), re]
                bi = st_scr[pl.ds(row, SSM_BATCH), im]
                hr, hi = a_r * hr - a_i * hi + br, a_r * hi + a_i * hr + bi
                st_scr[pl.ds(row, SSM_BATCH), re] = hr
                st_scr[pl.ds(row, SSM_BATCH), im] = hi
                return hr, hi

            hr, hi = lax.fori_loop(0, steps, step, (hr_scr[:, re], hi_scr[:, re]), unroll=SCAN_UNROLL)
            hr_scr[:, re] = hr
            hi_scr[:, re] = hi

    @pl.when(d == 0)
    def _():
        scan(True)

    @pl.when(d == 1)
    def _():
        scan(False)

    @pl.when(i >= n_ctx_tiles)
    def _():
        for k in range(2):
            y = (jnp.dot(st_scr[:, half_w * k:half_w * (k + 1)].astype(BF16), cm_ref[k, 0],
                         preferred_element_type=F32)
                 + jnp.dot(st_scr[:, N_STATE + half_w * k:N_STATE + half_w * (k + 1)].astype(BF16), cm_ref[k, 1],
                           preferred_element_type=F32))
            for c in range(2):
                yt_scr[2 * k + c] = y[:, LANES * c:LANES * (c + 1)]
        for b in range(SSM_BATCH):
            y_ref[b] = jnp.concatenate(
                [yt_scr[c, pl.ds(b, steps, stride=SSM_BATCH), :] for c in range(n_slabs)], axis=1)


def _ssm(u_c, u_l, bm, cm, a_re, a_im):
    bsz, n_ctx, _ = u_c.shape
    n = u_l.shape[1]
    t = SSM_STEPS
    nct = n_ctx // t
    nlt = n // t
    rows = t * SSM_BATCH

    def ctx_tile(d, i):
        return jnp.where(d == 0, jnp.minimum(i, nct - 1), jnp.maximum(nct - 1 - i, 0))

    def lat_tile(d, i):
        j = jnp.maximum(i - nct, 0)
        return jnp.where(d == 0, j, nlt - 1 - j)

    once = pl.Buffered(1)
    return pl.pallas_call(
        functools.partial(_ssm_body, n_ctx_tiles=nct),
        grid=(2, bsz // SSM_BATCH, nct + nlt),
        in_specs=[pl.BlockSpec((SSM_BATCH, t, D_SSM), lambda d, bh, i: (bh, ctx_tile(d, i), 0)),
                  pl.BlockSpec((SSM_BATCH, t, D_SSM), lambda d, bh, i: (bh, lat_tile(d, i), 0)),
                  pl.BlockSpec((None, 2, 256, N_STATE), lambda d, bh, i: (d, 0, 0, 0), pipeline_mode=once),
                  pl.BlockSpec((None, 2, 2, N_STATE // 2, 256), lambda d, bh, i: (d, 0, 0, 0, 0),
                               pipeline_mode=once),
                  pl.BlockSpec((None, SUBLANES, N_STATE), lambda d, bh, i: (d, 0, 0), pipeline_mode=once),
                  pl.BlockSpec((None, SUBLANES, N_STATE), lambda d, bh, i: (d, 0, 0), pipeline_mode=once)],
        out_specs=pl.BlockSpec((None, SSM_BATCH, t, D_SSM), lambda d, bh, i: (d, bh, lat_tile(d, i), 0)),
        out_shape=jax.ShapeDtypeStruct((2, bsz, n, D_SSM), F32),
        scratch_shapes=[pltpu.VMEM((D_SSM // LANES, rows, LANES), F32),
                        pltpu.VMEM((D_SSM // LANES, rows, LANES), F32),
                        pltpu.VMEM((rows, 2 * N_STATE), F32),
                        pltpu.VMEM((SSM_BATCH, N_STATE), F32),
                        pltpu.VMEM((SSM_BATCH, N_STATE), F32)],
        compiler_params=_params(3),
        name="ssm",
    )(u_c, u_l, bm, cm, a_re, a_im)


def _merge_body(x_ref, yf_ref, yb_ref, u_ref, at_ref,
                sh1_ref, sc1_ref, g1_ref, sh2_ref, sc2_ref, g2_ref, n1_ref, n2_ref,
                wg_ref, dv_ref, wglu_ref, bglu_ref, wba_ref, wbs_ref, wo_ref,
                w1_ref, b1_ref, w2_ref, b2_ref, o_ref):
    x = x_ref[...]
    hb = _rms_modulate(x, n1_ref[...], sc1_ref[...], sh1_ref[...]).astype(BF16)
    gates = jnp.dot(hb, wg_ref[...], preferred_element_type=F32)
    gate_a = jax.nn.sigmoid(gates[:, :D_MODEL])
    gate_s = jax.nn.sigmoid(gates[:, D_MODEL:])

    y = yf_ref[...] + yb_ref[...] + dv_ref[...] * u_ref[...]
    y = jax.nn.gelu(y)
    z = jnp.dot(y.astype(BF16), wglu_ref[...], preferred_element_type=F32) + bglu_ref[...]
    y = y * jax.nn.sigmoid(z)

    merged = (gate_a * jnp.dot(at_ref[...], wba_ref[...], preferred_element_type=F32)
              + gate_s * jnp.dot(y.astype(BF16), wbs_ref[...], preferred_element_type=F32))
    mix = jnp.dot(merged.astype(BF16), wo_ref[...], preferred_element_type=F32)
    x1 = x + g1_ref[...] * mix

    h2 = _rms_modulate(x1, n2_ref[...], sc2_ref[...], sh2_ref[...]).astype(BF16)
    acc = jnp.zeros_like(x1)
    for c in range(D_FF // FF_CHUNK):
        cols = slice(FF_CHUNK * c, FF_CHUNK * (c + 1))
        hid = jnp.dot(h2, w1_ref[:, cols], preferred_element_type=F32) + b1_ref[:, cols]
        hid = jnp.square(jnp.maximum(hid, 0.0))
        acc = acc + jnp.dot(hid.astype(BF16), w2_ref[cols, :], preferred_element_type=F32)
    o_ref[...] = x1 + g2_ref[...] * (acc + b2_ref[...])


def _merge(x, y2, u2, attn, mods, n1, n2, wg, dvec, wglu, bglu, wba, wbs, wo, w1, b1, w2, b2):
    bsz, n, _ = x.shape
    tm = MERGE_ROWS
    row_spec = pl.BlockSpec((None, tm, D_MODEL), lambda b, i: (b, i, 0))
    mod_specs = [pl.BlockSpec((None, None, 1, D_MODEL), functools.partial(lambda k, b, i: (b, k, 0, 0), k))
                 for k in range(6)]
    in_specs = [row_spec,
                pl.BlockSpec((None, None, tm, D_SSM), lambda b, i: (0, b, i, 0)),
                pl.BlockSpec((None, None, tm, D_SSM), lambda b, i: (1, b, i, 0)),
                pl.BlockSpec((None, tm, D_SSM), lambda b, i: (b, i, 0)),
                row_spec] + mod_specs
    consts = [n1, n2, wg, dvec, wglu, bglu, wba, wbs, wo, w1, b1, w2, b2]
    in_specs += [_const_spec(a.shape) for a in consts]
    return pl.pallas_call(
        _merge_body,
        grid=(bsz, n // tm),
        in_specs=in_specs,
        out_specs=row_spec,
        out_shape=jax.ShapeDtypeStruct(x.shape, x.dtype),
        compiler_params=_params(2),
        name="merge",
    )(x, y2, y2, u2, attn, *([mods] * 6), *consts)


def _rope_tables(n_tok):
    half = ROPE_AXIS_DIM // 2
    pos = jnp.arange(n_tok, dtype=jnp.int32)
    row = (pos // GRID_W).astype(F32)
    col = (pos % GRID_W).astype(F32)
    inv_freq = ROPE_THETA ** (-jnp.arange(half, dtype=F32) / half)
    ang_r = row[:, None] * inv_freq
    ang_c = col[:, None] * inv_freq
    cos = jnp.concatenate([jnp.cos(ang_r)] * 2 + [jnp.cos(ang_c)] * 2, axis=-1)
    sin = jnp.concatenate([-jnp.sin(ang_r), jnp.sin(ang_r), -jnp.sin(ang_c), jnp.sin(ang_c)], axis=-1)
    reps = LANES // HEAD_DIM
    return jnp.tile(cos, (1, reps)), jnp.tile(sin, (1, reps))


def kernel(x, c, ctx, c_ctx, w_mod, b_mod, norm1_g, norm2_g, w_in, q_norm_g, k_norm_g,
           ssm_lambda_re, ssm_lambda_im, ssm_log_dt, ssm_b_re, ssm_b_im, ssm_c_re, ssm_c_im,
           ssm_d, w_glu, b_glu, w_br_attn, w_br_ssm, w_out, w_mlp1, b_mlp1, w_mlp2, b_mlp2):
    bsz, n_tok, d_model = x.shape
    n_ctx = ctx.shape[1]
    assert d_model == D_MODEL and w_mod.shape[0] == 1, "single-layer block of width 1024 only"
    assert n_tok % PROJ_ROWS == 0 and n_ctx % min(PROJ_ROWS, n_ctx) == 0 and n_ctx % SSM_STEPS == 0
    assert bsz % SSM_BATCH == 0 and n_tok % GRID_W == 0

    pad = (-(bsz + 1)) % SUBLANES
    c_rows = jnp.concatenate([c, c_ctx[None, :], jnp.zeros((pad, D_MODEL), F32)], axis=0)
    mod = _modulation(c_rows, w_mod[0], b_mod[0][None, :])
    mod = mod.reshape(mod.shape[0], 6, 1, D_MODEL)

    w = w_in[0]
    w_k = w[:, OFF_K:OFF_V].reshape(D_MODEL, N_KV_HEADS, 1, HEAD_DIM)
    w_kdup = jnp.broadcast_to(w_k, (D_MODEL, N_KV_HEADS, 2, HEAD_DIM)).reshape(D_MODEL, 2 * OFF_V - 2 * OFF_K)
    w_u = w[:, OFF_U:OFF_GA]
    wm_lat = jnp.concatenate([w[:, :OFF_K], w_kdup, w_u], axis=1).astype(BF16)
    wm_ctx = jnp.concatenate([w_kdup, w_u], axis=1).astype(BF16)
    w_vt = w[:, OFF_V:OFF_U].T.astype(BF16)
    w_gate = w[:, OFF_GA:].astype(BF16)
    mavg = jnp.kron(jnp.eye(MXU_TILE // HEAD_DIM, dtype=F32),
                    jnp.full((HEAD_DIM, HEAD_DIM), 1.0 / HEAD_DIM, F32)).astype(BF16)
    gk = jnp.tile(k_norm_g[0], 2 * N_KV_HEADS)
    gq = jnp.tile(q_norm_g[0], N_HEADS) * (HEAD_DIM ** -0.5 * LOG2_E)
    gn_lat = jnp.concatenate([gq, gk])[None, :]
    gn_ctx = gk[None, :]
    cos, sin = _rope_tables(n_tok)
    n1 = norm1_g[0][None, :]

    q, k_l, vt_l, u_l = _project(x, mod, None, n1, wm_lat, w_vt, mavg, gn_lat, cos, sin)
    _, k_c, vt_c, u_c = _project(ctx, mod, bsz, n1, wm_ctx, w_vt, mavg, gn_ctx, None, None)

    attn = _attention(q, k_l, k_c, vt_l, vt_c)

    a_re, a_im, bb_re, bb_im = _discretize(ssm_lambda_re[0], ssm_lambda_im[0], ssm_log_dt[0],
                                           ssm_b_re[0], ssm_b_im[0])
    bm, cm, a_re8, a_im8 = _ssm_matrices(a_re, a_im, bb_re, bb_im, ssm_c_re[0], ssm_c_im[0])
    y = _ssm(u_c, u_l, bm, cm, a_re8, a_im8)

    return _merge(x, y, u_l, attn, mod, n1, norm2_g[0][None, :],
                  w_gate, ssm_d[0][None, :], w_glu[0].astype(BF16), b_glu[0][None, :],
                  w_br_attn[0].astype(BF16), w_br_ssm[0].astype(BF16), w_out[0].astype(BF16),
                  w_mlp1[0].astype(BF16), b_mlp1[0][None, :], w_mlp2[0].astype(BF16), b_mlp2[0][None, :])
```

```python
import functools

import jax
import jax.numpy as jnp
from jax import lax
from jax.experimental import pallas as pl
from jax.experimental.pallas import tpu as pltpu

F32 = jnp.float32
BF16 = jnp.bfloat16

D_MODEL = 1024
N_HEADS = 16
N_KV_HEADS = 4
HEAD_DIM = 64
KV_REP = N_HEADS // N_KV_HEADS
GRID_W = 64
ROPE_THETA = 10000.0
ROPE_AXIS_DIM = HEAD_DIM // 2
D_SSM = 512
SSM_GROUP = 16
N_SSM_GROUPS = 32
SSM_STATE = 64
N_STATE = N_SSM_GROUPS * SSM_STATE
D_FF = 4 * D_MODEL
NORM_EPS = 1e-6
OFF_K = 1024
OFF_V = 1280
OFF_U = 1536
OFF_GA = 2048

LANES = 128
SUBLANES = 8
MXU_TILE = 256
LOG2_E = 1.4426950408889634
VMEM_LIMIT_BYTES = 56 * 1024 * 1024

PROJ_ROWS = 2048
PROJ_SUB_ROWS = 256
ATTN_ROWS = 256
ATTN_KEYS = 256
ONES_ROWS = 16
MERGE_ROWS = 512
SSM_STEPS = 128
SSM_BATCH = SUBLANES
SCAN_LANES = 1024
SCAN_UNROLL = 8
FF_CHUNK = 1024


def _const_spec(shape):
    nd = len(shape)
    return pl.BlockSpec(shape, lambda *_: (0,) * nd, pipeline_mode=pl.Buffered(1))


def _params(n_axes):
    return pltpu.CompilerParams(dimension_semantics=("arbitrary",) * n_axes,
                                vmem_limit_bytes=VMEM_LIMIT_BYTES)


def _mod_body(c_ref, w_ref, b_ref, o_ref):
    c = c_ref[...]
    s = c * jax.nn.sigmoid(c)
    o_ref[...] = jnp.dot(s, w_ref[...], preferred_element_type=F32) + b_ref[...]


def _modulation(c_rows, w_mod, b_mod):
    rows = c_rows.shape[0]
    ncol = w_mod.shape[1]
    blk = D_MODEL
    return pl.pallas_call(
        _mod_body,
        grid=(ncol // blk,),
        in_specs=[pl.BlockSpec((rows, D_MODEL), lambda j: (0, 0)),
                  pl.BlockSpec((D_MODEL, blk), lambda j: (0, j)),
                  pl.BlockSpec((1, blk), lambda j: (0, j))],
        out_specs=pl.BlockSpec((rows, blk), lambda j: (0, j)),
        out_shape=jax.ShapeDtypeStruct((rows, ncol), F32),
        compiler_params=_params(1),
        name="mod",
    )(c_rows, w_mod, b_mod)


def _disc_body(lr_ref, li_ref, ldt_ref, br_ref, bi_ref, ar_ref, ai_ref, bbr_ref, bbi_ref):
    lr = lr_ref[...]
    li = li_ref[...]
    dt = jnp.exp(ldt_ref[...])
    mag = jnp.exp(lr * dt)
    ar = mag * jnp.cos(li * dt)
    ai = mag * jnp.sin(li * dt)
    nr = ar - 1.0
    den = lr * lr + li * li
    cr = (nr * lr + ai * li) / den
    ci = (ai * lr - nr * li) / den
    br = br_ref[...]
    bi = bi_ref[...]
    ar_ref[...] = ar
    ai_ref[...] = ai
    bbr_ref[...] = cr * br - ci * bi
    bbi_ref[...] = cr * bi + ci * br


def _discretize(lam_re, lam_im, log_dt, b_re, b_im):
    rows = 2 * N_SSM_GROUPS
    cols = SSM_STATE * SSM_GROUP
    rep = lambda a: jnp.repeat(a.reshape(rows, SSM_STATE), SSM_GROUP, axis=1)
    ldt = jnp.broadcast_to(log_dt.reshape(rows, 1), (rows, cols))
    spec = pl.BlockSpec((rows, cols), lambda: (0, 0))
    sds = jax.ShapeDtypeStruct((rows, cols), F32)
    ar, ai, bbr, bbi = pl.pallas_call(
        _disc_body,
        in_specs=[spec] * 5,
        out_specs=[spec] * 4,
        out_shape=[sds] * 4,
        name="disc",
    )(rep(lam_re), rep(lam_im), ldt, b_re.reshape(rows, cols), b_im.reshape(rows, cols))
    a_re = ar[:, ::SSM_GROUP].reshape(2, N_STATE)
    a_im = ai[:, ::SSM_GROUP].reshape(2, N_STATE)
    shape = (2, N_SSM_GROUPS, SSM_STATE, SSM_GROUP)
    return a_re, a_im, bbr.reshape(shape), bbi.reshape(shape)


def _ssm_matrices(a_re, a_im, bb_re, bb_im, c_re, c_im):
    def block_diag(t, row_width, col_width):
        tiled = jnp.tile(t, (1, 1, 1, 16))
        row_group = jnp.arange(tiled.shape[2]) // row_width
        col_group = jnp.arange(tiled.shape[3]) // col_width
        return jnp.where(row_group[:, None] == col_group[None, :], tiled, 0.0)

    def in_blocks(bb):
        t = bb.reshape(2, 2, 16, SSM_STATE, SSM_GROUP).transpose(0, 1, 2, 4, 3)
        return block_diag(t.reshape(2, 2, 16 * SSM_GROUP, SSM_STATE), SSM_GROUP, SSM_STATE)

    def out_blocks(c):
        t = c.reshape(2, 2, 16, SSM_GROUP, SSM_STATE).transpose(0, 1, 2, 4, 3)
        return block_diag(t.reshape(2, 2, 16 * SSM_STATE, SSM_GROUP), SSM_STATE, SSM_GROUP)

    bm = jnp.concatenate([in_blocks(bb_re), in_blocks(bb_im)], axis=-1).astype(BF16)
    cm = jnp.stack([out_blocks(c_re), out_blocks(-c_im)], axis=2).astype(BF16)
    bcast = lambda a: jnp.broadcast_to(a[:, None, :], (2, SUBLANES, N_STATE))
    return bm, cm, bcast(a_re), bcast(a_im)


def _rms_modulate(x, g, sc, sh):
    ms = jnp.mean(x * x, axis=-1, keepdims=True)
    xn = x * lax.rsqrt(ms + NORM_EPS) * g
    return xn * (1.0 + sc) + sh


def _proj_body(*refs, n_norm, rope):
    if rope:
        (x_ref, sh_ref, sc_ref, g_ref, wm_ref, wvt_ref, mavg_ref, gn_ref, cos_ref, sin_ref,
         q_ref, k_ref, vt_ref, u_ref) = refs
    else:
        (x_ref, sh_ref, sc_ref, g_ref, wm_ref, wvt_ref, mavg_ref, gn_ref,
         k_ref, vt_ref, u_ref) = refs
    rows = x_ref.shape[0]
    sub = min(rows, PROJ_SUB_ROWS)
    mavg = mavg_ref[...]
    width = mavg.shape[0]
    n_q_chunks = n_norm // LANES - k_ref.shape[0]
    if rope:
        lane = lax.broadcasted_iota(jnp.int32, (sub, LANES), 1)
        half = ROPE_AXIS_DIM // 2
        first_half = (lane % ROPE_AXIS_DIM) < half
    for r0 in range(0, rows, sub):
        rs = slice(r0, r0 + sub)
        h = _rms_modulate(x_ref[rs, :], g_ref[...], sc_ref[...], sh_ref[...])
        hb = h.astype(BF16)
        p = jnp.dot(hb, wm_ref[...], preferred_element_type=F32)
        qk = p[:, :n_norm]
        u_ref[rs, :] = p[:, n_norm:]
        sq = (qk * qk).astype(BF16)
        msq = jnp.concatenate(
            [jnp.dot(sq[:, width * j:width * (j + 1)], mavg, preferred_element_type=F32)
             for j in range(n_norm // width)], axis=1)
        vt = lax.dot_general(wvt_ref[...], hb, (((1,), (1,)), ((), ())), preferred_element_type=F32)
        for g in range(N_KV_HEADS):
            vt_ref[g, :, rs] = vt[HEAD_DIM * g:HEAD_DIM * (g + 1), :].astype(BF16)
        qkn = qk * lax.rsqrt(msq + NORM_EPS) * gn_ref[...]
        for c in range(n_norm // LANES):
            xc = qkn[:, LANES * c:LANES * (c + 1)]
            if rope:
                partner = jnp.where(first_half, pltpu.roll(xc, LANES - half, 1), pltpu.roll(xc, half, 1))
                xc = xc * cos_ref[rs, :] + partner * sin_ref[rs, :]
            if c < n_q_chunks:
                q_ref[c, rs, :] = xc.astype(BF16)
            else:
                k_ref[c - n_q_chunks, rs, :] = xc.astype(BF16)


def _project(x, mod, mod_row, g, wm, wvt, mavg, gn, cos, sin):
    bsz, n, _ = x.shape
    tm = min(PROJ_ROWS, n)
    rope = cos is not None
    n_norm = gn.shape[1]
    ncol = wm.shape[1]
    kv_w = N_KV_HEADS * HEAD_DIM
    q_chunks = D_MODEL // LANES
    row = (lambda b: b) if mod_row is None else (lambda b: mod_row)
    in_specs = [pl.BlockSpec((None, tm, D_MODEL), lambda i, b: (b, i, 0)),
                pl.BlockSpec((None, None, 1, D_MODEL), lambda i, b: (row(b), 0, 0, 0)),
                pl.BlockSpec((None, None, 1, D_MODEL), lambda i, b: (row(b), 1, 0, 0)),
                _const_spec((1, D_MODEL)),
                _const_spec((D_MODEL, ncol)),
                _const_spec((kv_w, D_MODEL)),
                _const_spec(mavg.shape),
                _const_spec((1, n_norm))]
    args = [x, mod, mod, g, wm, wvt, mavg, gn]
    out_specs = [pl.BlockSpec((None, N_KV_HEADS, tm, LANES), lambda i, b: (b, 0, i, 0)),
                 pl.BlockSpec((None, N_KV_HEADS, HEAD_DIM, tm), lambda i, b: (b, 0, 0, i)),
                 pl.BlockSpec((None, tm, D_SSM), lambda i, b: (b, i, 0))]
    out_shape = [jax.ShapeDtypeStruct((bsz, N_KV_HEADS, n, LANES), BF16),
                 jax.ShapeDtypeStruct((bsz, N_KV_HEADS, HEAD_DIM, n), BF16),
                 jax.ShapeDtypeStruct((bsz, n, D_SSM), F32)]
    if rope:
        in_specs += [pl.BlockSpec((tm, LANES), lambda i, b: (i, 0))] * 2
        args += [cos, sin]
        out_specs = [pl.BlockSpec((None, q_chunks, tm, LANES), lambda i, b: (b, 0, i, 0))] + out_specs
        out_shape = [jax.ShapeDtypeStruct((bsz, q_chunks, n, LANES), BF16)] + out_shape
    outs = pl.pallas_call(
        functools.partial(_proj_body, n_norm=n_norm, rope=rope),
        grid=(n // tm, bsz),
        in_specs=in_specs,
        out_specs=out_specs,
        out_shape=out_shape,
        compiler_params=_params(2),
        name="proj_latent" if rope else "proj_ctx",
    )(*args)
    return outs if rope else [None] + list(outs)


def _attn_body(q_ref, qn_ref, kl_ref, kc_ref, vtl_ref, vtc_ref, o_ref, s0_scr, s1_scr, m_scr, o_scr):
    n_ctx = kc_ref.shape[1]
    n_lat = kl_ref.shape[1]
    key_blocks = ([(lo, min(ATTN_KEYS, n_ctx - lo)) for lo in range(0, n_ctx, ATTN_KEYS)]
                  + [(n_ctx + lo, ATTN_KEYS) for lo in range(0, n_lat, ATTN_KEYS)])
    tq = q_ref.shape[1]
    lane = lax.broadcasted_iota(jnp.int32, (tq, LANES), 1)
    head_lanes = (lane < HEAD_DIM, lane >= HEAD_DIM)
    s_bufs = (s0_scr, s1_scr)
    nt = (((1,), (1,)), ((), ()))
    reload_base = jnp.minimum(pl.program_id(1), 0)

    def slab_max(s):
        return jnp.max(s.reshape(s.shape[0] // SUBLANES, SUBLANES, tq), axis=0)

    def scores(qc, g, r, slot):
        s_scr = s_bufs[slot]
        qm = jnp.where(head_lanes[r % 2], qc, jnp.zeros_like(qc))
        sc = lax.dot_general(kc_ref[g], qm, nt, preferred_element_type=F32)
        sl = lax.dot_general(kl_ref[g], qm, nt, preferred_element_type=F32)
        s_scr[:n_ctx, :] = sc
        s_scr[n_ctx:, :] = sl
        m_scr[slot] = jnp.maximum(slab_max(sc), slab_max(sl))

    def finish(g, r, slot):
        s_scr = s_bufs[slot]
        m = jnp.max(m_scr[slot], axis=0, keepdims=True)
        m8 = jnp.broadcast_to(m, (SUBLANES, tq))
        acc = jnp.zeros((HEAD_DIM + ONES_ROWS, tq), F32)
        for lo, size in key_blocks:
            row = pl.multiple_of(reload_base + lo, SUBLANES)
            s = s_scr[pl.ds(row, size), :].reshape(size // SUBLANES, SUBLANES, tq)
            pb = jnp.exp2(s - m8).reshape(size, tq).astype(BF16)
            if lo < n_ctx:
                vt = vtc_ref[g, :, lo:lo + size]
            else:
                vt = vtl_ref[g, :, lo - n_ctx:lo - n_ctx + size]
            ones_rows = jnp.ones((ONES_ROWS, size), BF16)
            acc = acc + jnp.dot(jnp.concatenate([vt, ones_rows], axis=0), pb, preferred_element_type=F32)
        o_scr[KV_REP * g + r] = acc

    @pl.when(pl.program_id(1) == 0)
    def _():
        scores(q_ref[0], 0, 0, 0)

    def group(g, carry):
        for r in range(KV_REP):
            if r + 1 < KV_REP:
                scores(q_ref[2 * g + (r + 1) // 2], g, r + 1, (r + 1) % 2)
            else:
                last = g == N_KV_HEADS - 1
                qc = jnp.where(last, qn_ref[0], q_ref[2 * jnp.minimum(g + 1, N_KV_HEADS - 1)])
                scores(qc, jnp.where(last, 0, g + 1), 0, (r + 1) % 2)
            finish(g, r, r % 2)
        return carry

    lax.fori_loop(0, N_KV_HEADS, group, 0)
    o_all = o_scr[...]
    o_all = o_all[:, :HEAD_DIM, :] / o_all[:, HEAD_DIM:HEAD_DIM + 1, :]
    o_ref[...] = o_all.reshape(N_HEADS * HEAD_DIM, tq).T.astype(BF16)


def _attention(q, k_l, k_c, vt_l, vt_c):
    bsz, q_chunks, n, _ = q.shape
    n_ctx = k_c.shape[2]
    tq = ATTN_ROWS
    whole = lambda shape: pl.BlockSpec((None,) + shape, lambda b, j: (b,) + (0,) * len(shape))
    return pl.pallas_call(
        _attn_body,
        grid=(bsz, n // tq),
        in_specs=[pl.BlockSpec((None, q_chunks, tq, LANES), lambda b, j: (b, 0, j, 0)),
                  pl.BlockSpec((None, 1, tq, LANES), lambda b, j: (b, 0, jnp.minimum(j + 1, n // tq - 1), 0)),
                  whole((N_KV_HEADS, n, LANES)),
                  whole((N_KV_HEADS, n_ctx, LANES)),
                  whole((N_KV_HEADS, HEAD_DIM, n)),
                  whole((N_KV_HEADS, HEAD_DIM, n_ctx))],
        out_specs=pl.BlockSpec((None, tq, D_MODEL), lambda b, j: (b, j, 0)),
        out_shape=jax.ShapeDtypeStruct((bsz, n, D_MODEL), BF16),
        scratch_shapes=[pltpu.VMEM((n_ctx + n, tq), F32),
                        pltpu.VMEM((n_ctx + n, tq), F32),
                        pltpu.VMEM((2, SUBLANES, tq), F32),
                        pltpu.VMEM((N_HEADS, HEAD_DIM + ONES_ROWS, tq), F32)],
        compiler_params=_params(2),
        name="attn",
    )(q, q, k_l, k_c, vt_l, vt_c)


def _ssm_body(uc_ref, ul_ref, bm_ref, cm_ref, ar_ref, ai_ref, y_ref,
              ut_scr, yt_scr, st_scr, hr_scr, hi_scr, *, n_ctx_tiles):
    d = pl.program_id(0)
    i = pl.program_id(2)
    steps = uc_ref.shape[1]
    half_w = N_STATE // 2
    n_slabs = D_SSM // LANES

    @pl.when(i == 0)
    def _():
        hr_scr[...] = jnp.zeros_like(hr_scr)
        hi_scr[...] = jnp.zeros_like(hi_scr)

    u = jnp.where(i < n_ctx_tiles, uc_ref[...], ul_ref[...])
    for b in range(SSM_BATCH):
        for c in range(n_slabs):
            ut_scr[c, pl.ds(b, steps, stride=SSM_BATCH), :] = u[b, :, LANES * c:LANES * (c + 1)]
    ub = jnp.concatenate([ut_scr[c] for c in range(n_slabs)], axis=1).astype(BF16)
    for k in range(2):
        r = jnp.dot(ub[:, 256 * k:256 * (k + 1)], bm_ref[k], preferred_element_type=F32)
        st_scr[:, half_w * k:half_w * (k + 1)] = r[:, :half_w]
        st_scr[:, N_STATE + half_w * k:N_STATE + half_w * (k + 1)] = r[:, half_w:]

    def scan(forward):
        for c in range(N_STATE // SCAN_LANES):
            re = slice(SCAN_LANES * c, SCAN_LANES * (c + 1))
            im = slice(N_STATE + SCAN_LANES * c, N_STATE + SCAN_LANES * (c + 1))
            a_r = ar_ref[:, re]
            a_i = ai_ref[:, re]

            def step(j, carry):
                hr, hi = carry
                t = j if forward else steps - 1 - j
                row = pl.multiple_of(t * SSM_BATCH, SSM_BATCH)
                br = st_scr[pl.ds(row, SSM_BATCH), re]
                bi = st_scr[pl.ds(row, SSM_BATCH), im]
                hr, hi = a_r * hr - a_i * hi + br, a_r * hi + a_i * hr + bi
                st_scr[pl.ds(row, SSM_BATCH), re] = hr
                st_scr[pl.ds(row, SSM_BATCH), im] = hi
                return hr, hi

            hr, hi = lax.fori_loop(0, steps, step, (hr_scr[:, re], hi_scr[:, re]), unroll=SCAN_UNROLL)
            hr_scr[:, re] = hr
            hi_scr[:, re] = hi

    @pl.when(d == 0)
    def _():
        scan(True)

    @pl.when(d == 1)
    def _():
        scan(False)

    @pl.when(i >= n_ctx_tiles)
    def _():
        for k in range(2):
            y = (jnp.dot(st_scr[:, half_w * k:half_w * (k + 1)].astype(BF16), cm_ref[k, 0],
                         preferred_element_type=F32)
                 + jnp.dot(st_scr[:, N_STATE + half_w * k:N_STATE + half_w * (k + 1)].astype(BF16), cm_ref[k, 1],
                           preferred_element_type=F32))
            for c in range(2):
                yt_scr[2 * k + c] = y[:, LANES * c:LANES * (c + 1)]
        for b in range(SSM_BATCH):
            y_ref[b] = jnp.concatenate(
                [yt_scr[c, pl.ds(b, steps, stride=SSM_BATCH), :] for c in range(n_slabs)], axis=1)


def _ssm(u_c, u_l, bm, cm, a_re, a_im):
    bsz, n_ctx, _ = u_c.shape
    n = u_l.shape[1]
    t = SSM_STEPS
    nct = n_ctx // t
    nlt = n // t
    rows = t * SSM_BATCH

    def ctx_tile(d, i):
        return jnp.where(d == 0, jnp.minimum(i, nct - 1), jnp.maximum(nct - 1 - i, 0))

    def lat_tile(d, i):
        j = jnp.maximum(i - nct, 0)
        return jnp.where(d == 0, j, nlt - 1 - j)

    once = pl.Buffered(1)
    return pl.pallas_call(
        functools.partial(_ssm_body, n_ctx_tiles=nct),
        grid=(2, bsz // SSM_BATCH, nct + nlt),
        in_specs=[pl.BlockSpec((SSM_BATCH, t, D_SSM), lambda d, bh, i: (bh, ctx_tile(d, i), 0)),
                  pl.BlockSpec((SSM_BATCH, t, D_SSM), lambda d, bh, i: (bh, lat_tile(d, i), 0)),
                  pl.BlockSpec((None, 2, 256, N_STATE), lambda d, bh, i: (d, 0, 0, 0), pipeline_mode=once),
                  pl.BlockSpec((None, 2, 2, N_STATE // 2, 256), lambda d, bh, i: (d, 0, 0, 0, 0),
                               pipeline_mode=once),
                  pl.BlockSpec((None, SUBLANES, N_STATE), lambda d, bh, i: (d, 0, 0), pipeline_mode=once),
                  pl.BlockSpec((None, SUBLANES, N_STATE), lambda d, bh, i: (d, 0, 0), pipeline_mode=once)],
        out_specs=pl.BlockSpec((None, SSM_BATCH, t, D_SSM), lambda d, bh, i: (d, bh, lat_tile(d, i), 0)),
        out_shape=jax.ShapeDtypeStruct((2, bsz, n, D_SSM), F32),
        scratch_shapes=[pltpu.VMEM((D_SSM // LANES, rows, LANES), F32),
                        pltpu.VMEM((D_SSM // LANES, rows, LANES), F32),
                        pltpu.VMEM((rows, 2 * N_STATE), F32),
                        pltpu.VMEM((SSM_BATCH, N_STATE), F32),
                        pltpu.VMEM((SSM_BATCH, N_STATE), F32)],
        compiler_params=_params(3),
        name="ssm",
    )(u_c, u_l, bm, cm, a_re, a_im)


def _merge_body(x_ref, yf_ref, yb_ref, u_ref, at_ref,
                sh1_ref, sc1_ref, g1_ref, sh2_ref, sc2_ref, g2_ref, n1_ref, n2_ref,
                wg_ref, dv_ref, wglu_ref, bglu_ref, wba_ref, wbs_ref, wo_ref,
                w1_ref, b1_ref, w2_ref, b2_ref, o_ref):
    x = x_ref[...]
    hb = _rms_modulate(x, n1_ref[...], sc1_ref[...], sh1_ref[...]).astype(BF16)
    gates = jnp.dot(hb, wg_ref[...], preferred_element_type=F32)
    gate_a = jax.nn.sigmoid(gates[:, :D_MODEL])
    gate_s = jax.nn.sigmoid(gates[:, D_MODEL:])

    y = yf_ref[...] + yb_ref[...] + dv_ref[...] * u_ref[...]
    y = jax.nn.gelu(y)
    z = jnp.dot(y.astype(BF16), wglu_ref[...], preferred_element_type=F32) + bglu_ref[...]
    y = y * jax.nn.sigmoid(z)

    merged = (gate_a * jnp.dot(at_ref[...], wba_ref[...], preferred_element_type=F32)
              + gate_s * jnp.dot(y.astype(BF16), wbs_ref[...], preferred_element_type=F32))
    mix = jnp.dot(merged.astype(BF16), wo_ref[...], preferred_element_type=F32)
    x1 = x + g1_ref[...] * mix

    h2 = _rms_modulate(x1, n2_ref[...], sc2_ref[...], sh2_ref[...]).astype(BF16)
    acc = jnp.zeros_like(x1)
    for c in range(D_FF // FF_CHUNK):
        cols = slice(FF_CHUNK * c, FF_CHUNK * (c + 1))
        hid = jnp.dot(h2, w1_ref[:, cols], preferred_element_type=F32) + b1_ref[:, cols]
        hid = jnp.square(jnp.maximum(hid, 0.0))
        acc = acc + jnp.dot(hid.astype(BF16), w2_ref[cols, :], preferred_element_type=F32)
    o_ref[...] = x1 + g2_ref[...] * (acc + b2_ref[...])


def _merge(x, y2, u2, attn, mods, n1, n2, wg, dvec, wglu, bglu, wba, wbs, wo, w1, b1, w2, b2):
    bsz, n, _ = x.shape
    tm = MERGE_ROWS
    row_spec = pl.BlockSpec((None, tm, D_MODEL), lambda b, i: (b, i, 0))
    mod_specs = [pl.BlockSpec((None, None, 1, D_MODEL), functools.partial(lambda k, b, i: (b, k, 0, 0), k))
                 for k in range(6)]
    in_specs = [row_spec,
                pl.BlockSpec((None, None, tm, D_SSM), lambda b, i: (0, b, i, 0)),
                pl.BlockSpec((None, None, tm, D_SSM), lambda b, i: (1, b, i, 0)),
                pl.BlockSpec((None, tm, D_SSM), lambda b, i: (b, i, 0)),
                row_spec] + mod_specs
    consts = [n1, n2, wg, dvec, wglu, bglu, wba, wbs, wo, w1, b1, w2, b2]
    in_specs += [_const_spec(a.shape) for a in consts]
    return pl.pallas_call(
        _merge_body,
        grid=(bsz, n // tm),
        in_specs=in_specs,
        out_specs=row_spec,
        out_shape=jax.ShapeDtypeStruct(x.shape, x.dtype),
        compiler_params=_params(2),
        name="merge",
    )(x, y2, y2, u2, attn, *([mods] * 6), *consts)


def _rope_tables(n_tok):
    half = ROPE_AXIS_DIM // 2
    pos = jnp.arange(n_tok, dtype=jnp.int32)
    row = (pos // GRID_W).astype(F32)
    col = (pos % GRID_W).astype(F32)
    inv_freq = ROPE_THETA ** (-jnp.arange(half, dtype=F32) / half)
    ang_r = row[:, None] * inv_freq
    ang_c = col[:, None] * inv_freq
    cos = jnp.concatenate([jnp.cos(ang_r)] * 2 + [jnp.cos(ang_c)] * 2, axis=-1)
    sin = jnp.concatenate([-jnp.sin(ang_r), jnp.sin(ang_r), -jnp.sin(ang_c), jnp.sin(ang_c)], axis=-1)
    reps = LANES // HEAD_DIM
    return jnp.tile(cos, (1, reps)), jnp.tile(sin, (1, reps))


def kernel(x, c, ctx, c_ctx, w_mod, b_mod, norm1_g, norm2_g, w_in, q_norm_g, k_norm_g,
           ssm_lambda_re, ssm_lambda_im, ssm_log_dt, ssm_b_re, ssm_b_im, ssm_c_re, ssm_c_im,
           ssm_d, w_glu, b_glu, w_br_attn, w_br_ssm, w_out, w_mlp1, b_mlp1, w_mlp2, b_mlp2):
    bsz, n_tok, d_model = x.shape
    n_ctx = ctx.shape[1]
    assert d_model == D_MODEL and w_mod.shape[0] == 1, "single-layer block of width 1024 only"
    assert n_tok % PROJ_ROWS == 0 and n_ctx % min(PROJ_ROWS, n_ctx) == 0 and n_ctx % SSM_STEPS == 0
    assert bsz % SSM_BATCH == 0 and n_tok % GRID_W == 0

    pad = (-(bsz + 1)) % SUBLANES
    c_rows = jnp.concatenate([c, c_ctx[None, :], jnp.zeros((pad, D_MODEL), F32)], axis=0)
    mod = _modulation(c_rows, w_mod[0], b_mod[0][None, :])
    mod = mod.reshape(mod.shape[0], 6, 1, D_MODEL)

    w = w_in[0]
    w_k = w[:, OFF_K:OFF_V].reshape(D_MODEL, N_KV_HEADS, 1, HEAD_DIM)
    w_kdup = jnp.broadcast_to(w_k, (D_MODEL, N_KV_HEADS, 2, HEAD_DIM)).reshape(D_MODEL, 2 * OFF_V - 2 * OFF_K)
    w_u = w[:, OFF_U:OFF_GA]
    wm_lat = jnp.concatenate([w[:, :OFF_K], w_kdup, w_u], axis=1).astype(BF16)
    wm_ctx = jnp.concatenate([w_kdup, w_u], axis=1).astype(BF16)
    w_vt = w[:, OFF_V:OFF_U].T.astype(BF16)
    w_gate = w[:, OFF_GA:].astype(BF16)
    mavg = jnp.kron(jnp.eye(MXU_TILE // HEAD_DIM, dtype=F32),
                    jnp.full((HEAD_DIM, HEAD_DIM), 1.0 / HEAD_DIM, F32)).astype(BF16)
    gk = jnp.tile(k_norm_g[0], 2 * N_KV_HEADS)
    gq = jnp.tile(q_norm_g[0], N_HEADS) * (HEAD_DIM ** -0.5 * LOG2_E)
    gn_lat = jnp.concatenate([gq, gk])[None, :]
    gn_ctx = gk[None, :]
    cos, sin = _rope_tables(n_tok)
    n1 = norm1_g[0][None, :]

    q, k_l, vt_l, u_l = _project(x, mod, None, n1, wm_lat, w_vt, mavg, gn_lat, cos, sin)
    _, k_c, vt_c, u_c = _project(ctx, mod, bsz, n1, wm_ctx, w_vt, mavg, gn_ctx, None, None)

    attn = _attention(q, k_l, k_c, vt_l, vt_c)

    a_re, a_im, bb_re, bb_im = _discretize(ssm_lambda_re[0], ssm_lambda_im[0], ssm_log_dt[0],
                                           ssm_b_re[0], ssm_b_im[0])
    bm, cm, a_re8, a_im8 = _ssm_matrices(a_re, a_im, bb_re, bb_im, ssm_c_re[0], ssm_c_im[0])
    y = _ssm(u_c, u_l, bm, cm, a_re8, a_im8)

    return _merge(x, y, u_l, attn, mod, n1, norm2_g[0][None, :],
                  w_gate, ssm_d[0][None, :], w_glu[0].astype(BF16), b_glu[0][None, :],
                  w_br_attn[0].astype(BF16), w_br_ssm[0].astype(BF16), w_out[0].astype(BF16),
                  w_mlp1[0].astype(BF16), b_mlp1[0][None, :], w_mlp2[0].astype(BF16), b_mlp2[0][None, :])
```

```python
import functools

import jax
import jax.numpy as jnp
from jax import lax
from jax.experimental import pallas as pl
from jax.experimental.pallas import tpu as pltpu

F32 = jnp.float32
BF16 = jnp.bfloat16

D_MODEL = 1024
N_HEADS = 16
N_KV_HEADS = 4
HEAD_DIM = 64
KV_REP = N_HEADS // N_KV_HEADS
GRID_W = 64
ROPE_THETA = 10000.0
ROPE_AXIS_DIM = HEAD_DIM // 2
D_SSM = 512
SSM_GROUP = 16
N_SSM_GROUPS = 32
SSM_STATE = 64
N_STATE = N_SSM_GROUPS * SSM_STATE
D_FF = 4 * D_MODEL
NORM_EPS = 1e-6
OFF_K = 1024
OFF_V = 1280
OFF_U = 1536
OFF_GA = 2048

LANES = 128
SUBLANES = 8
MXU_TILE = 256
LOG2_E = 1.4426950408889634
VMEM_LIMIT_BYTES = 56 * 1024 * 1024

PROJ_ROWS = 2048
PROJ_SUB_ROWS = 256
ATTN_ROWS = 256
ATTN_KEYS = 256
ONES_ROWS = 16
MERGE_ROWS = 512
SSM_STEPS = 128
SSM_BATCH = SUBLANES
SCAN_LANES = 1024
SCAN_UNROLL = 8
FF_CHUNK = 1024


def _const_spec(shape):
    nd = len(shape)
    return pl.BlockSpec(shape, lambda *_: (0,) * nd, pipeline_mode=pl.Buffered(1))


def _params(n_axes):
    return pltpu.CompilerParams(dimension_semantics=("arbitrary",) * n_axes,
                                vmem_limit_bytes=VMEM_LIMIT_BYTES)


def _mod_body(c_ref, w_ref, b_ref, o_ref):
    c = c_ref[...]
    s = c * jax.nn.sigmoid(c)
    o_ref[...] = jnp.dot(s, w_ref[...], preferred_element_type=F32) + b_ref[...]


def _modulation(c_rows, w_mod, b_mod):
    rows = c_rows.shape[0]
    ncol = w_mod.shape[1]
    blk = D_MODEL
    return pl.pallas_call(
        _mod_body,
        grid=(ncol // blk,),
        in_specs=[pl.BlockSpec((rows, D_MODEL), lambda j: (0, 0)),
                  pl.BlockSpec((D_MODEL, blk), lambda j: (0, j)),
                  pl.BlockSpec((1, blk), lambda j: (0, j))],
        out_specs=pl.BlockSpec((rows, blk), lambda j: (0, j)),
        out_shape=jax.ShapeDtypeStruct((rows, ncol), F32),
        compiler_params=_params(1),
        name="mod",
    )(c_rows, w_mod, b_mod)


def _disc_body(lr_ref, li_ref, ldt_ref, br_ref, bi_ref, ar_ref, ai_ref, bbr_ref, bbi_ref):
    lr = lr_ref[...]
    li = li_ref[...]
    dt = jnp.exp(ldt_ref[...])
    mag = jnp.exp(lr * dt)
    ar = mag * jnp.cos(li * dt)
    ai = mag * jnp.sin(li * dt)
    nr = ar - 1.0
    den = lr * lr + li * li
    cr = (nr * lr + ai * li) / den
    ci = (ai * lr - nr * li) / den
    br = br_ref[...]
    bi = bi_ref[...]
    ar_ref[...] = ar
    ai_ref[...] = ai
    bbr_ref[...] = cr * br - ci * bi
    bbi_ref[...] = cr * bi + ci * br


def _discretize(lam_re, lam_im, log_dt, b_re, b_im):
    rows = 2 * N_SSM_GROUPS
    cols = SSM_STATE * SSM_GROUP
    rep = lambda a: jnp.repeat(a.reshape(rows, SSM_STATE), SSM_GROUP, axis=1)
    ldt = jnp.broadcast_to(log_dt.reshape(rows, 1), (rows, cols))
    spec = pl.BlockSpec((rows, cols), lambda: (0, 0))
    sds = jax.ShapeDtypeStruct((rows, cols), F32)
    ar, ai, bbr, bbi = pl.pallas_call(
        _disc_body,
        in_specs=[spec] * 5,
        out_specs=[spec] * 4,
        out_shape=[sds] * 4,
        name="disc",
    )(rep(lam_re), rep(lam_im), ldt, b_re.reshape(rows, cols), b_im.reshape(rows, cols))
    a_re = ar[:, ::SSM_GROUP].reshape(2, N_STATE)
    a_im = ai[:, ::SSM_GROUP].reshape(2, N_STATE)
    shape = (2, N_SSM_GROUPS, SSM_STATE, SSM_GROUP)
    return a_re, a_im, bbr.reshape(shape), bbi.reshape(shape)


def _ssm_matrices(a_re, a_im, bb_re, bb_im, c_re, c_im):
    def block_diag(t, row_width, col_width):
        tiled = jnp.tile(t, (1, 1, 1, 16))
        row_group = jnp.arange(tiled.shape[2]) // row_width
        col_group = jnp.arange(tiled.shape[3]) // col_width
        return jnp.where(row_group[:, None] == col_group[None, :], tiled, 0.0)

    def in_blocks(bb):
        t = bb.reshape(2, 2, 16, SSM_STATE, SSM_GROUP).transpose(0, 1, 2, 4, 3)
        return block_diag(t.reshape(2, 2, 16 * SSM_GROUP, SSM_STATE), SSM_GROUP, SSM_STATE)

    def out_blocks(c):
        t = c.reshape(2, 2, 16, SSM_GROUP, SSM_STATE).transpose(0, 1, 2, 4, 3)
        return block_diag(t.reshape(2, 2, 16 * SSM_STATE, SSM_GROUP), SSM_STATE, SSM_GROUP)

    bm = jnp.concatenate([in_blocks(bb_re), in_blocks(bb_im)], axis=-1).astype(BF16)
    cm = jnp.stack([out_blocks(c_re), out_blocks(-c_im)], axis=2).astype(BF16)
    bcast = lambda a: jnp.broadcast_to(a[:, None, :], (2, SUBLANES, N_STATE))
    return bm, cm, bcast(a_re), bcast(a_im)


def _rms_modulate(x, g, sc, sh):
    ms = jnp.mean(x * x, axis=-1, keepdims=True)
    xn = x * lax.rsqrt(ms + NORM_EPS) * g
    return xn * (1.0 + sc) + sh


def _proj_body(*refs, n_norm, rope):
    if rope:
        (x_ref, sh_ref, sc_ref, g_ref, wm_ref, wvt_ref, mavg_ref, gn_ref, cos_ref, sin_ref,
         q_ref, k_ref, vt_ref, u_ref) = refs
    else:
        (x_ref, sh_ref, sc_ref, g_ref, wm_ref, wvt_ref, mavg_ref, gn_ref,
         k_ref, vt_ref, u_ref) = refs
    rows = x_ref.shape[0]
    sub = min(rows, PROJ_SUB_ROWS)
    mavg = mavg_ref[...]
    width = mavg.shape[0]
    n_q_chunks = n_norm // LANES - k_ref.shape[0]
    if rope:
        lane = lax.broadcasted_iota(jnp.int32, (sub, LANES), 1)
        half = ROPE_AXIS_DIM // 2
        first_half = (lane % ROPE_AXIS_DIM) < half
    for r0 in range(0, rows, sub):
        rs = slice(r0, r0 + sub)
        h = _rms_modulate(x_ref[rs, :], g_ref[...], sc_ref[...], sh_ref[...])
        hb = h.astype(BF16)
        p = jnp.dot(hb, wm_ref[...], preferred_element_type=F32)
        qk = p[:, :n_norm]
        u_ref[rs, :] = p[:, n_norm:]
        sq = (qk * qk).astype(BF16)
        msq = jnp.concatenate(
            [jnp.dot(sq[:, width * j:width * (j + 1)], mavg, preferred_element_type=F32)
             for j in range(n_norm // width)], axis=1)
        vt = lax.dot_general(wvt_ref[...], hb, (((1,), (1,)), ((), ())), preferred_element_type=F32)
        for g in range(N_KV_HEADS):
            vt_ref[g, :, rs] = vt[HEAD_DIM * g:HEAD_DIM * (g + 1), :].astype(BF16)
        qkn = qk * lax.rsqrt(msq + NORM_EPS) * gn_ref[...]
        for c in range(n_norm // LANES):
            xc = qkn[:, LANES * c:LANES * (c + 1)]
            if rope:
                partner = jnp.where(first_half, pltpu.roll(xc, LANES - half, 1), pltpu.roll(xc, half, 1))
                xc = xc * cos_ref[rs, :] + partner * sin_ref[rs, :]
            if c < n_q_chunks:
                q_ref[c, rs, :] = xc.astype(BF16)
            else:
                k_ref[c - n_q_chunks, rs, :] = xc.astype(BF16)


def _project(x, mod, mod_row, g, wm, wvt, mavg, gn, cos, sin):
    bsz, n, _ = x.shape
    tm = min(PROJ_ROWS, n)
    rope = cos is not None
    n_norm = gn.shape[1]
    ncol = wm.shape[1]
    kv_w = N_KV_HEADS * HEAD_DIM
    q_chunks = D_MODEL // LANES
    row = (lambda b: b) if mod_row is None else (lambda b: mod_row)
    in_specs = [pl.BlockSpec((None, tm, D_MODEL), lambda i, b: (b, i, 0)),
                pl.BlockSpec((None, None, 1, D_MODEL), lambda i, b: (row(b), 0, 0, 0)),
                pl.BlockSpec((None, None, 1, D_MODEL), lambda i, b: (row(b), 1, 0, 0)),
                _const_spec((1, D_MODEL)),
                _const_spec((D_MODEL, ncol)),
                _const_spec((kv_w, D_MODEL)),
                _const_spec(mavg.shape),
                _const_spec((1, n_norm))]
    args = [x, mod, mod, g, wm, wvt, mavg, gn]
    out_specs = [pl.BlockSpec((None, N_KV_HEADS, tm, LANES), lambda i, b: (b, 0, i, 0)),
                 pl.BlockSpec((None, N_KV_HEADS, HEAD_DIM, tm), lambda i, b: (b, 0, 0, i)),
                 pl.BlockSpec((None, tm, D_SSM), lambda i, b: (b, i, 0))]
    out_shape = [jax.ShapeDtypeStruct((bsz, N_KV_HEADS, n, LANES), BF16),
                 jax.ShapeDtypeStruct((bsz, N_KV_HEADS, HEAD_DIM, n), BF16),
                 jax.ShapeDtypeStruct((bsz, n, D_SSM), F32)]
    if rope:
        in_specs += [pl.BlockSpec((tm, LANES), lambda i, b: (i, 0))] * 2
        args += [cos, sin]
        out_specs = [pl.BlockSpec((None, q_chunks, tm, LANES), lambda i, b: (b, 0, i, 0))] + out_specs
        out_shape = [jax.ShapeDtypeStruct((bsz, q_chunks, n, LANES), BF16)] + out_shape
    outs = pl.pallas_call(
        functools.partial(_proj_body, n_norm=n_norm, rope=rope),
        grid=(n // tm, bsz),
        in_specs=in_specs,
        out_specs=out_specs,
        out_shape=out_shape,
        compiler_params=_params(2),
        name="proj_latent" if rope else "proj_ctx",
    )(*args)
    return outs if rope else [None] + list(outs)


def _attn_body(q_ref, qn_ref, kl_ref, kc_ref, vtl_ref, vtc_ref, o_ref, s0_scr, s1_scr, m_scr, o_scr):
    n_ctx = kc_ref.shape[1]
    n_lat = kl_ref.shape[1]
    key_blocks = ([(lo, min(ATTN_KEYS, n_ctx - lo)) for lo in range(0, n_ctx, ATTN_KEYS)]
                  + [(n_ctx + lo, ATTN_KEYS) for lo in range(0, n_lat, ATTN_KEYS)])
    tq = q_ref.shape[1]
    lane = lax.broadcasted_iota(jnp.int32, (tq, LANES), 1)
    head_lanes = (lane < HEAD_DIM, lane >= HEAD_DIM)
    s_bufs = (s0_scr, s1_scr)
    nt = (((1,), (1,)), ((), ()))
    reload_base = jnp.minimum(pl.program_id(1), 0)

    def slab_max(s):
        return jnp.max(s.reshape(s.shape[0] // SUBLANES, SUBLANES, tq), axis=0)

    def scores(qc, g, r, slot):
        s_scr = s_bufs[slot]
        qm = jnp.where(head_lanes[r % 2], qc, jnp.zeros_like(qc))
        sc = lax.dot_general(kc_ref[g], qm, nt, preferred_element_type=F32)
        sl = lax.dot_general(kl_ref[g], qm, nt, preferred_element_type=F32)
        s_scr[:n_ctx, :] = sc
        s_scr[n_ctx:, :] = sl
        m_scr[slot] = jnp.maximum(slab_max(sc), slab_max(sl))

    def finish(g, r, slot):
        s_scr = s_bufs[slot]
        m = jnp.max(m_scr[slot], axis=0, keepdims=True)
        m8 = jnp.broadcast_to(m, (SUBLANES, tq))
        acc = jnp.zeros((HEAD_DIM + ONES_ROWS, tq), F32)
        for lo, size in key_blocks:
            row = pl.multiple_of(reload_base + lo, SUBLANES)
            s = s_scr[pl.ds(row, size), :].reshape(size // SUBLANES, SUBLANES, tq)
            pb = jnp.exp2(s - m8).reshape(size, tq).astype(BF16)
            if lo < n_ctx:
                vt = vtc_ref[g, :, lo:lo + size]
            else:
                vt = vtl_ref[g, :, lo - n_ctx:lo - n_ctx + size]
            ones_rows = jnp.ones((ONES_ROWS, size), BF16)
            acc = acc + jnp.dot(jnp.concatenate([vt, ones_rows], axis=0), pb, preferred_element_type=F32)
        o_scr[KV_REP * g + r] = acc

    @pl.when(pl.program_id(1) == 0)
    def _():
        scores(q_ref[0], 0, 0, 0)

    def group(g, carry):
        for r in range(KV_REP):
            if r + 1 < KV_REP:
                scores(q_ref[2 * g + (r + 1) // 2], g, r + 1, (r + 1) % 2)
            else:
                last = g == N_KV_HEADS - 1
                qc = jnp.where(last, qn_ref[0], q_ref[2 * jnp.minimum(g + 1, N_KV_HEADS - 1)])
                scores(qc, jnp.where(last, 0, g + 1), 0, (r + 1) % 2)
            finish(g, r, r % 2)
        return carry

    lax.fori_loop(0, N_KV_HEADS, group, 0)
    o_all = o_scr[...]
    o_all = o_all[:, :HEAD_DIM, :] * (1.0 / o_all[:, HEAD_DIM:HEAD_DIM + 1, :])
    o_ref[...] = o_all.reshape(N_HEADS * HEAD_DIM, tq).T.astype(BF16)


def _attention(q, k_l, k_c, vt_l, vt_c):
    bsz, q_chunks, n, _ = q.shape
    n_ctx = k_c.shape[2]
    tq = ATTN_ROWS
    whole = lambda shape: pl.BlockSpec((None,) + shape, lambda b, j: (b,) + (0,) * len(shape))
    return pl.pallas_call(
        _attn_body,
        grid=(bsz, n // tq),
        in_specs=[pl.BlockSpec((None, q_chunks, tq, LANES), lambda b, j: (b, 0, j, 0)),
                  pl.BlockSpec((None, 1, tq, LANES), lambda b, j: (b, 0, jnp.minimum(j + 1, n // tq - 1), 0)),
                  whole((N_KV_HEADS, n, LANES)),
                  whole((N_KV_HEADS, n_ctx, LANES)),
                  whole((N_KV_HEADS, HEAD_DIM, n)),
                  whole((N_KV_HEADS, HEAD_DIM, n_ctx))],
        out_specs=pl.BlockSpec((None, tq, D_MODEL), lambda b, j: (b, j, 0)),
        out_shape=jax.ShapeDtypeStruct((bsz, n, D_MODEL), BF16),
        scratch_shapes=[pltpu.VMEM((n_ctx + n, tq), F32),
                        pltpu.VMEM((n_ctx + n, tq), F32),
                        pltpu.VMEM((2, SUBLANES, tq), F32),
                        pltpu.VMEM((N_HEADS, HEAD_DIM + ONES_ROWS, tq), F32)],
        compiler_params=_params(2),
        name="attn",
    )(q, q, k_l, k_c, vt_l, vt_c)


def _ssm_body(uc_ref, ul_ref, bm_ref, cm_ref, ar_ref, ai_ref, y_ref,
              ut_scr, yt_scr, st_scr, hr_scr, hi_scr, *, n_ctx_tiles):
    d = pl.program_id(0)
    i = pl.program_id(2)
    steps = uc_ref.shape[1]
    half_w = N_STATE // 2
    n_slabs = D_SSM // LANES

    @pl.when(i == 0)
    def _():
        hr_scr[...] = jnp.zeros_like(hr_scr)
        hi_scr[...] = jnp.zeros_like(hi_scr)

    u = jnp.where(i < n_ctx_tiles, uc_ref[...], ul_ref[...])
    for b in range(SSM_BATCH):
        for c in range(n_slabs):
            ut_scr[c, pl.ds(b, steps, stride=SSM_BATCH), :] = u[b, :, LANES * c:LANES * (c + 1)]
    ub = jnp.concatenate([ut_scr[c] for c in range(n_slabs)], axis=1).astype(BF16)
    for k in range(2):
        r = jnp.dot(ub[:, 256 * k:256 * (k + 1)], bm_ref[k], preferred_element_type=F32)
        st_scr[:, half_w * k:half_w * (k + 1)] = r[:, :half_w]
        st_scr[:, N_STATE + half_w * k:N_STATE + half_w * (k + 1)] = r[:, half_w:]

    def scan(forward):
        for c in range(N_STATE // SCAN_LANES):
            re = slice(SCAN_LANES * c, SCAN_LANES * (c + 1))
            im = slice(N_STATE + SCAN_LANES * c, N_STATE + SCAN_LANES * (c + 1))
            a_r = ar_ref[:, re]
            a_i = ai_ref[:, re]

            def step(j, carry):
                hr, hi = carry
                t = j if forward else steps - 1 - j
                row = pl.multiple_of(t * SSM_BATCH, SSM_BATCH)
                br = st_scr[pl.ds(row, SSM_BATCH), re]
                bi = st_scr[pl.ds(row, SSM_BATCH), im]
                hr, hi = a_r * hr - a_i * hi + br, a_r * hi + a_i * hr + bi
                st_scr[pl.ds(row, SSM_BATCH), re] = hr
                st_scr[pl.ds(row, SSM_BATCH), im] = hi
                return hr, hi

            hr, hi = lax.fori_loop(0, steps, step, (hr_scr[:, re], hi_scr[:, re]), unroll=SCAN_UNROLL)
            hr_scr[:, re] = hr
            hi_scr[:, re] = hi

    @pl.when(d == 0)
    def _():
        scan(True)

    @pl.when(d == 1)
    def _():
        scan(False)

    @pl.when(i >= n_ctx_tiles)
    def _():
        for k in range(2):
            y = (jnp.dot(st_scr[:, half_w * k:half_w * (k + 1)].astype(BF16), cm_ref[k, 0],
                         preferred_element_type=F32)
                 + jnp.dot(st_scr[:, N_STATE + half_w * k:N_STATE + half_w * (k + 1)].astype(BF16), cm_ref[k, 1],
                           preferred_element_type=F32))
            for c in range(2):
                yt_scr[2 * k + c] = y[:, LANES * c:LANES * (c + 1)]
        for b in range(SSM_BATCH):
            y_ref[b] = jnp.concatenate(
                [yt_scr[c, pl.ds(b, steps, stride=SSM_BATCH), :] for c in range(n_slabs)], axis=1)


def _ssm(u_c, u_l, bm, cm, a_re, a_im):
    bsz, n_ctx, _ = u_c.shape
    n = u_l.shape[1]
    t = SSM_STEPS
    nct = n_ctx // t
    nlt = n // t
    rows = t * SSM_BATCH

    def ctx_tile(d, i):
        return jnp.where(d == 0, jnp.minimum(i, nct - 1), jnp.maximum(nct - 1 - i, 0))

    def lat_tile(d, i):
        j = jnp.maximum(i - nct, 0)
        return jnp.where(d == 0, j, nlt - 1 - j)

    once = pl.Buffered(1)
    return pl.pallas_call(
        functools.partial(_ssm_body, n_ctx_tiles=nct),
        grid=(2, bsz // SSM_BATCH, nct + nlt),
        in_specs=[pl.BlockSpec((SSM_BATCH, t, D_SSM), lambda d, bh, i: (bh, ctx_tile(d, i), 0)),
                  pl.BlockSpec((SSM_BATCH, t, D_SSM), lambda d, bh, i: (bh, lat_tile(d, i), 0)),
                  pl.BlockSpec((None, 2, 256, N_STATE), lambda d, bh, i: (d, 0, 0, 0), pipeline_mode=once),
                  pl.BlockSpec((None, 2, 2, N_STATE // 2, 256), lambda d, bh, i: (d, 0, 0, 0, 0),
                               pipeline_mode=once),
                  pl.BlockSpec((None, SUBLANES, N_STATE), lambda d, bh, i: (d, 0, 0), pipeline_mode=once),
                  pl.BlockSpec((None, SUBLANES, N_STATE), lambda d, bh, i: (d, 0, 0), pipeline_mode=once)],
        out_specs=pl.BlockSpec((None, SSM_BATCH, t, D_SSM), lambda d, bh, i: (d, bh, lat_tile(d, i), 0)),
        out_shape=jax.ShapeDtypeStruct((2, bsz, n, D_SSM), F32),
        scratch_shapes=[pltpu.VMEM((D_SSM // LANES, rows, LANES), F32),
                        pltpu.VMEM((D_SSM // LANES, rows, LANES), F32),
                        pltpu.VMEM((rows, 2 * N_STATE), F32),
                        pltpu.VMEM((SSM_BATCH, N_STATE), F32),
                        pltpu.VMEM((SSM_BATCH, N_STATE), F32)],
        compiler_params=_params(3),
        name="ssm",
    )(u_c, u_l, bm, cm, a_re, a_im)


def _merge_body(x_ref, yf_ref, yb_ref, u_ref, at_ref,
                sh1_ref, sc1_ref, g1_ref, sh2_ref, sc2_ref, g2_ref, n1_ref, n2_ref,
                wg_ref, dv_ref, wglu_ref, bglu_ref, wba_ref, wbs_ref, wo_ref,
                w1_ref, b1_ref, w2_ref, b2_ref, o_ref):
    x = x_ref[...]
    hb = _rms_modulate(x, n1_ref[...], sc1_ref[...], sh1_ref[...]).astype(BF16)
    gates = jnp.dot(hb, wg_ref[...], preferred_element_type=F32)
    gate_a = jax.nn.sigmoid(gates[:, :D_MODEL])
    gate_s = jax.nn.sigmoid(gates[:, D_MODEL:])

    y = yf_ref[...] + yb_ref[...] + dv_ref[...] * u_ref[...]
    y = jax.nn.gelu(y)
    z = jnp.dot(y.astype(BF16), wglu_ref[...], preferred_element_type=F32) + bglu_ref[...]
    y = y * jax.nn.sigmoid(z)

    merged = (gate_a * jnp.dot(at_ref[...], wba_ref[...], preferred_element_type=F32)
              + gate_s * jnp.dot(y.astype(BF16), wbs_ref[...], preferred_element_type=F32))
    mix = jnp.dot(merged.astype(BF16), wo_ref[...], preferred_element_type=F32)
    x1 = x + g1_ref[...] * mix

    h2 = _rms_modulate(x1, n2_ref[...], sc2_ref[...], sh2_ref[...]).astype(BF16)
    acc = jnp.zeros_like(x1)
    for c in range(D_FF // FF_CHUNK):
        cols = slice(FF_CHUNK * c, FF_CHUNK * (c + 1))
        hid = jnp.dot(h2, w1_ref[:, cols], preferred_element_type=F32) + b1_ref[:, cols]
        hid = jnp.square(jnp.maximum(hid, 0.0))
        acc = acc + jnp.dot(hid.astype(BF16), w2_ref[cols, :], preferred_element_type=F32)
    o_ref[...] = x1 + g2_ref[...] * (acc + b2_ref[...])


def _merge(x, y2, u2, attn, mods, n1, n2, wg, dvec, wglu, bglu, wba, wbs, wo, w1, b1, w2, b2):
    bsz, n, _ = x.shape
    tm = MERGE_ROWS
    row_spec = pl.BlockSpec((None, tm, D_MODEL), lambda b, i: (b, i, 0))
    mod_specs = [pl.BlockSpec((None, None, 1, D_MODEL), functools.partial(lambda k, b, i: (b, k, 0, 0), k))
                 for k in range(6)]
    in_specs = [row_spec,
                pl.BlockSpec((None, None, tm, D_SSM), lambda b, i: (0, b, i, 0)),
                pl.BlockSpec((None, None, tm, D_SSM), lambda b, i: (1, b, i, 0)),
                pl.BlockSpec((None, tm, D_SSM), lambda b, i: (b, i, 0)),
                row_spec] + mod_specs
    consts = [n1, n2, wg, dvec, wglu, bglu, wba, wbs, wo, w1, b1, w2, b2]
    in_specs += [_const_spec(a.shape) for a in consts]
    return pl.pallas_call(
        _merge_body,
        grid=(bsz, n // tm),
        in_specs=in_specs,
        out_specs=row_spec,
        out_shape=jax.ShapeDtypeStruct(x.shape, x.dtype),
        compiler_params=_params(2),
        name="merge",
    )(x, y2, y2, u2, attn, *([mods] * 6), *consts)


def _rope_tables(n_tok):
    half = ROPE_AXIS_DIM // 2
    pos = jnp.arange(n_tok, dtype=jnp.int32)
    row = (pos // GRID_W).astype(F32)
    col = (pos % GRID_W).astype(F32)
    inv_freq = ROPE_THETA ** (-jnp.arange(half, dtype=F32) / half)
    ang_r = row[:, None] * inv_freq
    ang_c = col[:, None] * inv_freq
    cos = jnp.concatenate([jnp.cos(ang_r)] * 2 + [jnp.cos(ang_c)] * 2, axis=-1)
    sin = jnp.concatenate([-jnp.sin(ang_r), jnp.sin(ang_r), -jnp.sin(ang_c), jnp.sin(ang_c)], axis=-1)
    reps = LANES // HEAD_DIM
    return jnp.tile(cos, (1, reps)), jnp.tile(sin, (1, reps))


def kernel(x, c, ctx, c_ctx, w_mod, b_mod, norm1_g, norm2_g, w_in, q_norm_g, k_norm_g,
           ssm_lambda_re, ssm_lambda_im, ssm_log_dt, ssm_b_re, ssm_b_im, ssm_c_re, ssm_c_im,
           ssm_d, w_glu, b_glu, w_br_attn, w_br_ssm, w_out, w_mlp1, b_mlp1, w_mlp2, b_mlp2):
    bsz, n_tok, d_model = x.shape
    n_ctx = ctx.shape[1]
    assert d_model == D_MODEL and w_mod.shape[0] == 1, "single-layer block of width 1024 only"
    assert n_tok % PROJ_ROWS == 0 and n_ctx % min(PROJ_ROWS, n_ctx) == 0 and n_ctx % SSM_STEPS == 0
    assert bsz % SSM_BATCH == 0 and n_tok % GRID_W == 0

    pad = (-(bsz + 1)) % SUBLANES
    c_rows = jnp.concatenate([c, c_ctx[None, :], jnp.zeros((pad, D_MODEL), F32)], axis=0)
    mod = _modulation(c_rows, w_mod[0], b_mod[0][None, :])
    mod = mod.reshape(mod.shape[0], 6, 1, D_MODEL)

    w = w_in[0]
    w_k = w[:, OFF_K:OFF_V].reshape(D_MODEL, N_KV_HEADS, 1, HEAD_DIM)
    w_kdup = jnp.broadcast_to(w_k, (D_MODEL, N_KV_HEADS, 2, HEAD_DIM)).reshape(D_MODEL, 2 * OFF_V - 2 * OFF_K)
    w_u = w[:, OFF_U:OFF_GA]
    wm_lat = jnp.concatenate([w[:, :OFF_K], w_kdup, w_u], axis=1).astype(BF16)
    wm_ctx = jnp.concatenate([w_kdup, w_u], axis=1).astype(BF16)
    w_vt = w[:, OFF_V:OFF_U].T.astype(BF16)
    w_gate = w[:, OFF_GA:].astype(BF16)
    mavg = jnp.kron(jnp.eye(MXU_TILE // HEAD_DIM, dtype=F32),
                    jnp.full((HEAD_DIM, HEAD_DIM), 1.0 / HEAD_DIM, F32)).astype(BF16)
    gk = jnp.tile(k_norm_g[0], 2 * N_KV_HEADS)
    gq = jnp.tile(q_norm_g[0], N_HEADS) * (HEAD_DIM ** -0.5 * LOG2_E)
    gn_lat = jnp.concatenate([gq, gk])[None, :]
    gn_ctx = gk[None, :]
    cos, sin = _rope_tables(n_tok)
    n1 = norm1_g[0][None, :]

    q, k_l, vt_l, u_l = _project(x, mod, None, n1, wm_lat, w_vt, mavg, gn_lat, cos, sin)
    _, k_c, vt_c, u_c = _project(ctx, mod, bsz, n1, wm_ctx, w_vt, mavg, gn_ctx, None, None)

    attn = _attention(q, k_l, k_c, vt_l, vt_c)

    a_re, a_im, bb_re, bb_im = _discretize(ssm_lambda_re[0], ssm_lambda_im[0], ssm_log_dt[0],
                                           ssm_b_re[0], ssm_b_im[0])
    bm, cm, a_re8, a_im8 = _ssm_matrices(a_re, a_im, bb_re, bb_im, ssm_c_re[0], ssm_c_im[0])
    y = _ssm(u_c, u_l, bm, cm, a_re8, a_im8)

    return _merge(x, y, u_l, attn, mod, n1, norm2_g[0][None, :],
                  w_gate, ssm_d[0][None, :], w_glu[0].astype(BF16), b_glu[0][None, :],
                  w_br_attn[0].astype(BF16), w_br_ssm[0].astype(BF16), w_out[0].astype(BF16),
                  w_mlp1[0].astype(BF16), b_mlp1[0][None, :], w_mlp2[0].astype(BF16), b_mlp2[0][None, :])
```
